```python
import math
import jax
import jax.numpy as jnp
from jax import lax
import numpy as np

D_MODEL = 1024
BATCH = 32
SEQ = 2048
DEPTH = 1
DEC_BATCH = 128
DEC_SEQ = 1
PAST_LEN = 8192
PAGE_SIZE = 128

H_A = 4
DH_A = 128
MOBA_BLOCK = 256
MOBA_TOPK = 3
H_B = 4
DH_B = 64
H_M = 4
DH_M = 128
N_MEM = 256
ROPE_THETA = 500000.0
ROT_FRACTION = 4
RMS_EPS = 1e-6
SUBLN_EPS = 1e-5
Q_BLOCK = 128
N_EXPERTS = 32
MOE_TOPK = 4
D_FF = 1024
SWIGLU_ALPHA = 1.702
SWIGLU_LIMIT = 7.0
MOE_BLOCK = 256

kernel_name = 'moba_diff_memory_gated_hybrid_step'


def rms_norm(x, g, eps=RMS_EPS):
    xf = x.astype(jnp.float32)
    y = xf * lax.rsqrt(jnp.mean(xf * xf, axis=-1, keepdims=True) + eps)
    return (y * g.astype(jnp.float32)).astype(x.dtype)


def rope_partial(x, pos):
    d = x.shape[-1]
    rot = d // ROT_FRACTION
    half = rot // 2
    inv = ROPE_THETA ** (-(jnp.arange(half, dtype=jnp.float32) * 2.0 / rot))
    ang = pos.astype(jnp.float32)[:, None] * inv[None, :]
    shape = (1, pos.shape[0]) + (1,) * (x.ndim - 3) + (half,)
    cos = jnp.cos(ang).reshape(shape)
    sin = jnp.sin(ang).reshape(shape)
    xf = x.astype(jnp.float32)
    x1 = xf[..., :half]
    x2 = xf[..., half:rot]
    out = jnp.concatenate([x1 * cos - x2 * sin, x2 * cos + x1 * sin, xf[..., rot:]], axis=-1)
    return out.astype(x.dtype)


def mixer_inputs(x, pos, w_in, g_norm_mix, g_qa, g_ka, g_qb, g_kb, g_qm):
    b, l = x.shape[0], x.shape[1]
    h = rms_norm(x, g_norm_mix)
    z = jnp.einsum('bld,de->ble', h, w_in)
    wa = H_A * DH_A
    wb = H_B * 2 * DH_B
    wm = H_M * DH_M
    cuts = [wa, 2 * wa, 3 * wa, 3 * wa + wb, 3 * wa + 2 * wb, 3 * wa + 3 * wb, 3 * wa + 3 * wb + wm]
    qa, ka, va, qb, kb, vb, qm, zg = jnp.split(z, cuts, axis=-1)
    qa = rope_partial(rms_norm(qa.reshape(b, l, H_A, DH_A), g_qa), pos)
    ka = rope_partial(rms_norm(ka.reshape(b, l, H_A, DH_A), g_ka), pos)
    va = va.reshape(b, l, H_A, DH_A)
    qb = rope_partial(rms_norm(qb.reshape(b, l, H_B, 2, DH_B), g_qb), pos)
    kb = rope_partial(rms_norm(kb.reshape(b, l, H_B, 2, DH_B), g_kb), pos)
    vb = vb.reshape(b, l, H_B, 2 * DH_B)
    qm = rms_norm(qm.reshape(b, l, H_M, DH_M), g_qm)
    return qa, ka, va, qb, kb, vb, qm, zg


def memory_kv(mem, g_norm_mem, w_mem_kv, g_km):
    b, m = mem.shape[0], mem.shape[1]
    kv = jnp.einsum('bmd,de->bme', rms_norm(mem, g_norm_mem), w_mem_kv)
    k, v = jnp.split(kv, 2, axis=-1)
    return rms_norm(k.reshape(b, m, H_M, DH_M), g_km), v.reshape(b, m, H_M, DH_M)


def mem_attend(q, k, v):
    s = jnp.einsum('blhd,bmhd->bhlm', q, k, preferred_element_type=jnp.float32) * (DH_M ** -0.5)
    p = jax.nn.softmax(s, axis=-1)
    return jnp.einsum('bhlm,bmhd->blhd', p.astype(v.dtype), v)


def pad_blocks(k):
    l = k.shape[0]
    nb = max(-(-l // MOBA_BLOCK), MOBA_TOPK)
    return jnp.pad(k, ((0, nb * MOBA_BLOCK - l),) + ((0, 0),) * (k.ndim - 1))


def block_means(kp):
    nb = kp.shape[0] // MOBA_BLOCK
    return kp.reshape((nb, MOBA_BLOCK) + kp.shape[1:]).astype(jnp.float32).mean(axis=1)


def moba_attend(q, k, v, kmean, q_pos):
    n_h, d = q.shape[1], q.shape[2]
    nq = q.shape[0]
    nb = kmean.shape[0]
    q_blk = q_pos // MOBA_BLOCK
    gate = jnp.einsum('qhd,nhd->hqn', q.astype(jnp.float32), kmean)
    fully_past = jnp.arange(nb)[None, None, :] < q_blk[None, :, None]
    gate = jnp.where(fully_past, gate, -jnp.inf)
    _, sel = lax.top_k(gate, MOBA_TOPK)
    sel_ok = jnp.arange(MOBA_TOPK)[None, :] < jnp.minimum(q_blk, MOBA_TOPK)[:, None]
    blk_ids = jnp.concatenate([sel, jnp.broadcast_to(q_blk[None, :, None], (n_h, nq, 1))], axis=-1)
    blk_ok = jnp.concatenate([sel_ok, jnp.ones((nq, 1), dtype=bool)], axis=-1)
    kh = k.reshape(nb, MOBA_BLOCK, n_h, d).transpose(2, 0, 1, 3)
    vh = v.reshape(nb, MOBA_BLOCK, n_h, d).transpose(2, 0, 1, 3)
    h_ar = jnp.arange(n_h)[:, None, None]
    kg = kh[h_ar, blk_ids]
    vg = vh[h_ar, blk_ids]
    s = jnp.einsum('qhd,hqnkd->hqnk', q, kg, preferred_element_type=jnp.float32) * (d ** -0.5)
    kpos = blk_ids[..., None] * MOBA_BLOCK + jnp.arange(MOBA_BLOCK)
    mask = (kpos <= q_pos[None, :, None, None]) & blk_ok[None, :, :, None]
    s = jnp.where(mask, s, -jnp.inf)
    p = jax.nn.softmax(s.reshape(n_h, nq, -1), axis=-1).reshape(s.shape)
    return jnp.einsum('hqnk,hqnkd->qhd', p.astype(vg.dtype), vg)


def diff_attend(q, k, v, q_pos, lam):
    s = jnp.einsum('qhcd,lhcd->hcql', q, k, preferred_element_type=jnp.float32) * (DH_B ** -0.5)
    mask = jnp.arange(k.shape[0])[None, :] <= q_pos[:, None]
    s = jnp.where(mask[None, None], s, -jnp.inf)
    a = jax.nn.softmax(s, axis=-1)
    w = a[:, 0] - lam * a[:, 1]
    return jnp.einsum('hql,lhe->qhe', w.astype(v.dtype), v)


def prompt_attention(qa, ka, va, qb, kb, vb, pos, lam):
    l = qa.shape[1]
    n_c = l // Q_BLOCK
    pos_c = pos.reshape(n_c, Q_BLOCK)

    def one_seq(args):
        qa1, ka1, va1, qb1, kb1, vb1 = args
        kpa = pad_blocks(ka1)
        vpa = pad_blocks(va1)
        kmean = block_means(kpa)

        def one_chunk(c):
            qa_c, qb_c, p_c = c
            return (moba_attend(qa_c, kpa, vpa, kmean, p_c), diff_attend(qb_c, kb1, vb1, p_c, lam))

        oa1, ob1 = lax.map(one_chunk, (qa1.reshape((n_c, Q_BLOCK) + qa1.shape[1:]),
                                       qb1.reshape((n_c, Q_BLOCK) + qb1.shape[1:]), pos_c))
        return oa1.reshape((l,) + oa1.shape[2:]), ob1.reshape((l,) + ob1.shape[2:])

    return lax.map(one_seq, (qa, ka, va, qb, kb, vb))


def sample_attention(qa, ka, va, qb, kb, vb, pos, lam, cache_ka, cache_va, cache_kb, cache_vb, page_table, layer):
    def one_seq(args):
        qa1, ka1, va1, qb1, kb1, vb1, pt = args

        def past(cache):
            pages = cache[layer, pt]
            return pages.reshape((-1,) + pages.shape[2:])

        kfa = pad_blocks(jnp.concatenate([past(cache_ka), ka1], axis=0))
        vfa = pad_blocks(jnp.concatenate([past(cache_va), va1], axis=0))
        oa1 = moba_attend(qa1, kfa, vfa, block_means(kfa), pos)
        kfb = jnp.concatenate([past(cache_kb), kb1], axis=0)
        vfb = jnp.concatenate([past(cache_vb), vb1], axis=0)
        ob1 = diff_attend(qb1, kfb, vfb, pos, lam)
        return oa1, ob1

    return lax.map(one_seq, (qa, ka, va, qb, kb, vb, page_table))


def merge_branches(x, oa, ob, om, zg, b_gate, w_br_a, w_br_b, w_br_m, w_out):
    b, l = x.shape[0], x.shape[1]
    g = jax.nn.sigmoid((zg + b_gate).astype(jnp.float32)).astype(x.dtype)
    ga, gb, gm = jnp.split(g, 3, axis=-1)
    merged = (ga * (oa.reshape(b, l, -1) @ w_br_a) + gb * (ob.reshape(b, l, -1) @ w_br_b)
              + gm * (om.reshape(b, l, -1) @ w_br_m))
    return x + merged @ w_out


def moe_ffn(h, w_router, b_router, w1, b1, w2, b2):
    n_tok, d = h.shape
    logits = (h @ w_router).astype(jnp.float32) + b_router.astype(jnp.float32)
    top_val, top_idx = lax.top_k(logits, MOE_TOPK)
    gate = jax.nn.softmax(top_val, axis=-1)
    n_as = n_tok * MOE_TOPK
    e_flat = top_idx.reshape(n_as)
    tok_flat = jnp.repeat(jnp.arange(n_tok, dtype=jnp.int32), MOE_TOPK)
    g_flat = gate.reshape(n_as)
    order = jnp.argsort(e_flat, stable=True)
    e_sorted = e_flat[order]
    counts = jnp.bincount(e_flat, length=N_EXPERTS)
    padded = (counts + MOE_BLOCK - 1) // MOE_BLOCK * MOE_BLOCK
    pad_end = jnp.cumsum(padded)
    pad_start = pad_end - padded
    start = jnp.cumsum(counts) - counts
    dest = pad_start[e_sorted] + jnp.arange(n_as, dtype=jnp.int32) - start[e_sorted]
    n_blocks = -(-n_as // MOE_BLOCK) + N_EXPERTS
    n_rows = n_blocks * MOE_BLOCK
    row_tok = jnp.full((n_rows,), n_tok, dtype=jnp.int32).at[dest].set(tok_flat[order])
    row_gate = jnp.zeros((n_rows,), dtype=jnp.float32).at[dest].set(g_flat[order])
    blk_expert = jnp.minimum(jnp.searchsorted(pad_end, jnp.arange(n_blocks, dtype=jnp.int32) * MOE_BLOCK, side='right'),
                             N_EXPERTS - 1)
    h_pad = jnp.concatenate([h, jnp.zeros((1, d), h.dtype)], axis=0)

    def expert_block(args):
        toks, e = args
        u = h_pad[toks] @ w1[e] + b1[e]
        u_glu, u_lin = jnp.split(u, 2, axis=-1)
        u_glu = jnp.minimum(u_glu, SWIGLU_LIMIT)
        u_lin = jnp.clip(u_lin, -SWIGLU_LIMIT, SWIGLU_LIMIT)
        act = u_glu * jax.nn.sigmoid(SWIGLU_ALPHA * u_glu) * (u_lin + 1.0)
        return act @ w2[e] + b2[e]

    rows = lax.map(expert_block, (row_tok.reshape(n_blocks, MOE_BLOCK), blk_expert))
    rows = rows.reshape(n_rows, d) * row_gate[:, None].astype(rows.dtype)
    return jax.ops.segment_sum(rows, row_tok, num_segments=n_tok + 1)[:n_tok]


def setup_inputs(seed: int = 0) -> dict:
    key = jax.random.key(seed)
    ks = iter(jax.random.split(key, 48))
    f32 = jnp.float32

    def rnd(shape, scale=1.0):
        return jax.random.normal(next(ks), shape, f32) * scale

    def gain(shape):
        return 1.0 + 0.05 * rnd(shape)

    n_pages = PAST_LEN // PAGE_SIZE
    n_pool = (DEC_BATCH * n_pages * 5) // 4
    w_in_cols = 3 * H_A * DH_A + 3 * H_B * 2 * DH_B + H_M * DH_M + 3 * D_MODEL
    return {
        'x_prompt': rnd((BATCH, SEQ, D_MODEL)),
        'x_sample': rnd((DEC_BATCH, DEC_SEQ, D_MODEL)),
        'mem_prompt': rnd((BATCH, N_MEM, D_MODEL)),
        'cache_moba_k': rnd((DEPTH, n_pool, PAGE_SIZE, H_A, DH_A)),
        'cache_moba_v': rnd((DEPTH, n_pool, PAGE_SIZE, H_A, DH_A)),
        'cache_diff_k': rnd((DEPTH, n_pool, PAGE_SIZE, H_B, 2, DH_B)),
        'cache_diff_v': rnd((DEPTH, n_pool, PAGE_SIZE, H_B, 2 * DH_B)),
        'cache_mem_k': rnd((DEPTH, DEC_BATCH, N_MEM, H_M, DH_M)),
        'cache_mem_v': rnd((DEPTH, DEC_BATCH, N_MEM, H_M, DH_M)),
        'page_table': jax.random.permutation(next(ks), n_pool)[:DEC_BATCH * n_pages].reshape(DEC_BATCH, n_pages).astype(jnp.int32),
        'w_in': rnd((DEPTH, D_MODEL, w_in_cols), D_MODEL ** -0.5),
        'b_gate': rnd((DEPTH, 3 * D_MODEL), 0.01),
        'g_norm_mix': gain((DEPTH, D_MODEL)),
        'g_qa': gain((DEPTH, DH_A)),
        'g_ka': gain((DEPTH, DH_A)),
        'g_qb': gain((DEPTH, DH_B)),
        'g_kb': gain((DEPTH, DH_B)),
        'lambda_q1': rnd((DEPTH, DH_B), 0.1),
        'lambda_k1': rnd((DEPTH, DH_B), 0.1),
        'lambda_q2': rnd((DEPTH, DH_B), 0.1),
        'lambda_k2': rnd((DEPTH, DH_B), 0.1),
        'g_subln': gain((DEPTH, 2 * DH_B)),
        'g_norm_mem': gain((DEPTH, D_MODEL)),
        'w_mem_kv': rnd((DEPTH, D_MODEL, 2 * H_M * DH_M), D_MODEL ** -0.5),
        'g_qm': gain((DEPTH, DH_M)),
        'g_km': gain((DEPTH, DH_M)),
        'w_br_a': rnd((DEPTH, H_A * DH_A, D_MODEL), (H_A * DH_A) ** -0.5),
        'w_br_b': rnd((DEPTH, H_B * 2 * DH_B, D_MODEL), (H_B * 2 * DH_B) ** -0.5),
        'w_br_m': rnd((DEPTH, H_M * DH_M, D_MODEL), (H_M * DH_M) ** -0.5),
        'w_out': rnd((DEPTH, D_MODEL, D_MODEL), D_MODEL ** -0.5),
        'g_norm_ffn': gain((DEPTH, D_MODEL)),
        'w_router': rnd((DEPTH, D_MODEL, N_EXPERTS), D_MODEL ** -0.5),
        'b_router': rnd((DEPTH, N_EXPERTS), 0.01),
        'w1': rnd((DEPTH, N_EXPERTS, D_MODEL, 2 * D_FF), D_MODEL ** -0.5),
        'b1': rnd((DEPTH, N_EXPERTS, 2 * D_FF), 0.01),
        'w2': rnd((DEPTH, N_EXPERTS, D_FF, D_MODEL), D_FF ** -0.5),
        'b2': rnd((DEPTH, N_EXPERTS, D_MODEL), 0.01),
    }


def reference(x_prompt, x_sample, mem_prompt, cache_moba_k, cache_moba_v, cache_diff_k, cache_diff_v,
              cache_mem_k, cache_mem_v, page_table, w_in, b_gate, g_norm_mix, g_qa, g_ka, g_qb, g_kb,
              lambda_q1, lambda_k1, lambda_q2, lambda_k2, g_subln, g_norm_mem, w_mem_kv, g_qm, g_km,
              w_br_a, w_br_b, w_br_m, w_out, g_norm_ffn, w_router, b_router, w1, b1, w2, b2):
    f32 = jnp.float32
    n_prompt_tok = x_prompt.shape[0] * x_prompt.shape[1]
    pos_p = jnp.arange(x_prompt.shape[1], dtype=jnp.int32)
    past_len = page_table.shape[1] * cache_moba_k.shape[2]
    pos_s = past_len + jnp.arange(x_sample.shape[1], dtype=jnp.int32)
    xp, xs = x_prompt, x_sample
    moba_k_p, moba_v_p, diff_k_p, diff_v_p, mem_k_p, mem_v_p = [], [], [], [], [], []
    moba_k_s, moba_v_s, diff_k_s, diff_v_s = [], [], [], []
    for layer in range(DEPTH):
        lam_init = 0.8 - 0.6 * math.exp(-0.3 * layer)
        lam = (jnp.exp(jnp.sum(lambda_q1[layer].astype(f32) * lambda_k1[layer].astype(f32)))
               - jnp.exp(jnp.sum(lambda_q2[layer].astype(f32) * lambda_k2[layer].astype(f32))) + lam_init)
        proj = (w_in[layer], g_norm_mix[layer], g_qa[layer], g_ka[layer], g_qb[layer], g_kb[layer], g_qm[layer])
        qa_p, ka_p, va_p, qb_p, kb_p, vb_p, qm_p, zg_p = mixer_inputs(xp, pos_p, *proj)
        qa_s, ka_s, va_s, qb_s, kb_s, vb_s, qm_s, zg_s = mixer_inputs(xs, pos_s, *proj)
        mk_p, mv_p = memory_kv(mem_prompt, g_norm_mem[layer], w_mem_kv[layer], g_km[layer])
        oa_p, ob_p = prompt_attention(qa_p, ka_p, va_p, qb_p, kb_p, vb_p, pos_p, lam)
        oa_s, ob_s = sample_attention(qa_s, ka_s, va_s, qb_s, kb_s, vb_s, pos_s, lam, cache_moba_k, cache_moba_v,
                                      cache_diff_k, cache_diff_v, page_table, layer)
        om_p = mem_attend(qm_p, mk_p, mv_p)
        om_s = mem_attend(qm_s, cache_mem_k[layer], cache_mem_v[layer])
        ob_p = rms_norm(ob_p, g_subln[layer], SUBLN_EPS) * (1.0 - lam_init)
        ob_s = rms_norm(ob_s, g_subln[layer], SUBLN_EPS) * (1.0 - lam_init)
        merge_w = (b_gate[layer], w_br_a[layer], w_br_b[layer], w_br_m[layer], w_out[layer])
        xp = merge_branches(xp, oa_p, ob_p, om_p, zg_p, *merge_w)
        xs = merge_branches(xs, oa_s, ob_s, om_s, zg_s, *merge_w)
        tok = jnp.concatenate([xp.reshape(-1, D_MODEL), xs.reshape(-1, D_MODEL)], axis=0)
        tok = tok + moe_ffn(rms_norm(tok, g_norm_ffn[layer]), w_router[layer], b_router[layer],
                            w1[layer], b1[layer], w2[layer], b2[layer])
        xp = tok[:n_prompt_tok].reshape(x_prompt.shape)
        xs = tok[n_prompt_tok:].reshape(x_sample.shape)
        moba_k_p.append(ka_p)
        moba_v_p.append(va_p)
        diff_k_p.append(kb_p)
        diff_v_p.append(vb_p)
        mem_k_p.append(mk_p)
        mem_v_p.append(mv_p)
        moba_k_s.append(ka_s)
        moba_v_s.append(va_s)
        diff_k_s.append(kb_s)
        diff_v_s.append(vb_s)
    return (xp, xs, jnp.stack(moba_k_p), jnp.stack(moba_v_p), jnp.stack(diff_k_p), jnp.stack(diff_v_p),
            jnp.stack(mem_k_p), jnp.stack(mem_v_p), jnp.stack(moba_k_s), jnp.stack(moba_v_s),
            jnp.stack(diff_k_s), jnp.stack(diff_v_s))
```

```python
import functools
import math

import jax
import jax.numpy as jnp
from jax import lax
from jax.experimental import pallas as pl
from jax.experimental.pallas import tpu as pltpu

F32 = jnp.float32
BF16 = jnp.bfloat16
I32 = jnp.int32

N_HEADS = 4
HEAD_W = 128
SEC = N_HEADS * HEAD_W
DH_B = 64
MOBA_BLOCK = 256
MOBA_TOPK = 3
ROPE_THETA = 500000.0
ROT_FRACTION = 4
RMS_EPS = 1e-6
SUBLN_EPS = 1e-5
LAM_INIT = 0.8 - 0.6 * math.exp(-0.3 * 0)
MOE_TOPK = 4
MOE_BLOCK = 256
SWIGLU_ALPHA = 1.702
SWIGLU_LIMIT = 7.0
PAGES_PER_STEP = 8
NEG_INF = float("-inf")
VMEM_LIMIT = 56 * 1024 * 1024

_NT = (((1,), (1,)), ((), ()))


def _nt(a, b):
    return lax.dot_general(a, b, _NT, preferred_element_type=F32)


def _dot(a, b):
    return jnp.dot(a, b, preferred_element_type=F32)


def _rms(x, eps):
    return x * lax.rsqrt(jnp.mean(x * x, axis=-1, keepdims=True) + eps)


def _headnorm128(z, g):
    return _rms(z, RMS_EPS) * g


def _headnorm64(z, g):
    zz = z * z
    lo = lax.broadcasted_iota(I32, z.shape, 1) < DH_B
    tot = jnp.sum(zz, axis=-1, keepdims=True)
    s_lo = jnp.sum(jnp.where(lo, zz, 0.0), axis=-1, keepdims=True)
    s_hi = tot - s_lo
    r = jnp.where(lo, lax.rsqrt(s_lo * (1.0 / DH_B) + RMS_EPS), lax.rsqrt(s_hi * (1.0 / DH_B) + RMS_EPS))
    return z * r * g


def _rope(y, c, s_up, s_dn, half):
    return y * c + pltpu.roll(y, HEAD_W - half, 1) * s_up + pltpu.roll(y, half, 1) * s_dn


def _rope_tables(pos, d):
    rot = d // ROT_FRACTION
    half = rot // 2
    inv = ROPE_THETA ** (-(jnp.arange(half, dtype=F32) * 2.0 / rot))
    ang = pos.astype(F32)[:, None] * inv[None, :]
    cos, sin = jnp.cos(ang), jnp.sin(ang)
    p = pos.shape[0]
    c = jnp.concatenate([cos, cos, jnp.ones((p, d - rot), F32)], axis=-1)
    s_up = jnp.concatenate([-sin, jnp.zeros((p, d - half), F32)], axis=-1)
    s_dn = jnp.concatenate([jnp.zeros((p, half), F32), sin, jnp.zeros((p, d - rot), F32)], axis=-1)
    rep = HEAD_W // d
    return tuple(jnp.tile(t, (1, rep)) for t in (c, s_up, s_dn))


def _proj_kernel(x_ref, w_ref, gmix_ref, bg_ref, gqa_ref, gka_ref, gqb_ref, gkb_ref, gqm_ref,
                 ca_ref, sua_ref, sda_ref, cb_ref, sub_ref, sdb_ref,
                 qa_o, ka_o, va_o, qb_o, kb_o, vb_o, qm_o, g_o):
    h = (_rms(x_ref[...], RMS_EPS) * gmix_ref[...]).astype(BF16)

    def sec(k):
        return _dot(h, w_ref[:, k * SEC:(k + 1) * SEC])

    def per_head(z, fn, out):
        for hh in range(N_HEADS):
            sl = slice(hh * HEAD_W, (hh + 1) * HEAD_W)
            out[:, sl] = fn(z[:, sl]).astype(out.dtype)

    def rope_a(y):
        return _rope(y, ca_ref[...], sua_ref[...], sda_ref[...], HEAD_W // ROT_FRACTION // 2)

    def rope_b(y):
        return _rope(y, cb_ref[...], sub_ref[...], sdb_ref[...], DH_B // ROT_FRACTION // 2)

    per_head(sec(0), lambda z: rope_a(_headnorm128(z, gqa_ref[...])), qa_o)
    per_head(sec(1), lambda z: rope_a(_headnorm128(z, gka_ref[...])), ka_o)
    va_o[...] = sec(2)
    per_head(sec(3), lambda z: rope_b(_headnorm64(z, gqb_ref[...])), qb_o)
    per_head(sec(4), lambda z: rope_b(_headnorm64(z, gkb_ref[...])), kb_o)
    vb_o[...] = sec(5)
    per_head(sec(6), lambda z: _headnorm128(z, gqm_ref[...]), qm_o)
    n_gate_sec = g_o.shape[1] // SEC
    for k in range(n_gate_sec):
        z = sec(7 + k) + bg_ref[:, k * SEC:(k + 1) * SEC]
        g_o[:, k * SEC:(k + 1) * SEC] = (1.0 / (1.0 + jnp.exp(-z))).astype(BF16)


def _const_spec(shape):
    return pl.BlockSpec(shape, lambda *_: (0,) * len(shape), pipeline_mode=pl.Buffered(1))


def _project(x2d, tm, n_pos_blocks, w16, gmix, bgate, gqa, gka, gqb2, gkb2, gqm, tabs_a, tabs_b):
    n, d = x2d.shape
    ncol = w16.shape[1]
    n_gate = ncol - 7 * SEC
    row = lambda w: pl.BlockSpec((tm, w), lambda i: (i, 0))
    tab = pl.BlockSpec((tm, HEAD_W), lambda i: (i % n_pos_blocks, 0))
    in_specs = [row(d), _const_spec((d, ncol)), _const_spec((1, d)), _const_spec((1, n_gate))]
    in_specs += [_const_spec((1, HEAD_W))] * 5 + [tab] * 6
    out_shape = [jax.ShapeDtypeStruct((n, SEC), dt) for dt in (BF16, F32, F32, BF16, F32, F32, BF16)]
    out_shape.append(jax.ShapeDtypeStruct((n, n_gate), BF16))
    out_specs = [row(SEC)] * 7 + [row(n_gate)]
    return pl.pallas_call(
        _proj_kernel, grid=(n // tm,), in_specs=in_specs, out_specs=out_specs, out_shape=out_shape,
        compiler_params=pltpu.CompilerParams(dimension_semantics=("parallel",), vmem_limit_bytes=VMEM_LIMIT),
        name="proj",
    )(x2d, w16, gmix, bgate, gqa, gka, gqb2, gkb2, gqm, *tabs_a, *tabs_b)


def _memkv_kernel(x_ref, w_ref, gn_ref, gkm_ref, k_o, v_o):
    h = (_rms(x_ref[...], RMS_EPS) * gn_ref[...]).astype(BF16)
    zk = _dot(h, w_ref[:, :SEC])
    for hh in range(N_HEADS):
        sl = slice(hh * HEAD_W, (hh + 1) * HEAD_W)
        k_o[:, sl] = _headnorm128(zk[:, sl], gkm_ref[...])
    v_o[...] = _dot(h, w_ref[:, SEC:])


def _memory_kv(mem2d, tm, w16, gn, gkm):
    n, d = mem2d.shape
    row = lambda w: pl.BlockSpec((tm, w), lambda i: (i, 0))
    return pl.pallas_call(
        _memkv_kernel, grid=(n // tm,),
        in_specs=[row(d), _const_spec((d, 2 * SEC)), _const_spec((1, d)), _const_spec((1, HEAD_W))],
        out_specs=[row(SEC), row(SEC)],
        out_shape=[jax.ShapeDtypeStruct((n, SEC), F32)] * 2,
        compiler_params=pltpu.CompilerParams(dimension_semantics=("parallel",)),
        name="memkv",
    )(mem2d, w16, gn, gkm)


def _softmax_start(s, v16):
    m = jnp.max(s, axis=-1, keepdims=True)
    p = jnp.exp(s - m)
    return m, jnp.sum(p, axis=-1, keepdims=True), _dot(p.astype(BF16), v16)


def _softmax_update(s, v16, m, l, acc):
    m_new = jnp.maximum(m, jnp.max(s, axis=-1, keepdims=True))
    alpha = jnp.exp(m - m_new)
    p = jnp.exp(s - m_new)
    return m_new, alpha * l + jnp.sum(p, axis=-1, keepdims=True), alpha * acc + _dot(p.astype(BF16), v16)


def _subln(o, g):
    return _rms(o, SUBLN_EPS) * g * (1.0 - LAM_INIT)


def _attn_prompt_kernel(lam_ref, qa_ref, qb_ref, qm_ref, ka_ref, va_ref, kb_ref, vb_ref, mk_ref, mv_ref,
                        gsub_ref, oa_o, ob_o, om_o, ka16, va16, kb16, vb16, mk16, mv16, kmean16):
    qi = pl.program_id(2)
    seq = ka_ref.shape[0]
    tq = qa_ref.shape[0]
    nb = seq // MOBA_BLOCK
    lam = lam_ref[0]

    @pl.when(qi == 0)
    def _():
        ka = ka_ref[...]
        ka16[...] = ka.astype(BF16)
        va16[...] = va_ref[...].astype(BF16)
        kb16[...] = kb_ref[...].astype(BF16)
        vb16[...] = vb_ref[...].astype(BF16)
        mk16[...] = mk_ref[...].astype(BF16)
        mv16[...] = mv_ref[...].astype(BF16)
        km = jnp.mean(ka.reshape(nb, MOBA_BLOCK, HEAD_W), axis=1)
        kmean16[...] = jnp.concatenate([km, jnp.zeros((HEAD_W - nb, HEAD_W), F32)], axis=0).astype(BF16)

    qa = qa_ref[...]
    qb = qb_ref[...]
    lane = lax.broadcasted_iota(I32, (tq, HEAD_W), 1)
    scale_a = HEAD_W ** -0.5
    scale_b = DH_B ** -0.5

    g = jnp.where(lane < qi, _nt(qa, kmean16[...]), NEG_INF)
    sel = jnp.zeros((tq, HEAD_W), F32)
    for _ in range(MOBA_TOPK):
        mx = jnp.max(g, axis=-1, keepdims=True)
        idx = jnp.min(jnp.where(g == mx, lane, HEAD_W), axis=-1, keepdims=True)
        pick = (lane == idx) & (mx > NEG_INF)
        sel = jnp.where(pick, 1.0, sel)
        g = jnp.where(pick, NEG_INF, g)

    qbf = qb.astype(F32)
    qb0 = jnp.where(lane < DH_B, qbf, 0.0).astype(BF16)
    qb1 = jnp.where(lane >= DH_B, qbf, 0.0).astype(BF16)

    causal = lax.broadcasted_iota(I32, (tq, tq), 1) <= lax.broadcasted_iota(I32, (tq, tq), 0)
    d0 = pl.multiple_of(qi * tq, tq)
    kb_d = kb16[pl.ds(d0, tq), :]
    vb_d = vb16[pl.ds(d0, tq), :]
    st_a = _softmax_start(jnp.where(causal, _nt(qa, ka16[pl.ds(d0, tq), :]) * scale_a, NEG_INF),
                          va16[pl.ds(d0, tq), :])
    st_0 = _softmax_start(jnp.where(causal, _nt(qb0, kb_d) * scale_b, NEG_INF), vb_d)
    st_1 = _softmax_start(jnp.where(causal, _nt(qb1, kb_d) * scale_b, NEG_INF), vb_d)

    def past_block(n, carry):
        st_a, st_0, st_1 = carry
        off = pl.multiple_of(n * tq, tq)
        chosen = jnp.sum(jnp.where(lane == n, sel, 0.0), axis=-1, keepdims=True) > 0.5
        sa = jnp.where(chosen, _nt(qa, ka16[pl.ds(off, tq), :]) * scale_a, NEG_INF)
        st_a = _softmax_update(sa, va16[pl.ds(off, tq), :], *st_a)
        kb_n = kb16[pl.ds(off, tq), :]
        vb_n = vb16[pl.ds(off, tq), :]
        st_0 = _softmax_update(_nt(qb0, kb_n) * scale_b, vb_n, *st_0)
        st_1 = _softmax_update(_nt(qb1, kb_n) * scale_b, vb_n, *st_1)
        return st_a, st_0, st_1

    st_a, st_0, st_1 = lax.fori_loop(0, qi, past_block, (st_a, st_0, st_1))

    oa_o[...] = (st_a[2] / st_a[1]).astype(BF16)
    ob = st_0[2] / st_0[1] - lam * (st_1[2] / st_1[1])
    ob_o[...] = _subln(ob, gsub_ref[...]).astype(BF16)

    sm = _nt(qm_ref[...], mk16[...]) * scale_a
    _, lm, accm = _softmax_start(sm, mv16[...])
    om_o[...] = (accm / lm).astype(BF16)


def _attention_prompt(lam, qa, qb, qm, ka, va, kb, vb, mk, mv, gsub, batch, seq, n_mem):
    r3 = lambda a, rows: a.reshape(batch, rows, SEC)
    tq = MOBA_BLOCK
    qspec = pl.BlockSpec((None, tq, HEAD_W), lambda b, h, q: (b, q, h))
    kspec = pl.BlockSpec((None, seq, HEAD_W), lambda b, h, q: (b, 0, h))
    mspec = pl.BlockSpec((None, n_mem, HEAD_W), lambda b, h, q: (b, 0, h))
    outs = pl.pallas_call(
        _attn_prompt_kernel, grid=(batch, N_HEADS, seq // tq),
        in_specs=[pl.BlockSpec(memory_space=pltpu.SMEM), qspec, qspec, qspec, kspec, kspec, kspec, kspec,
                  mspec, mspec, pl.BlockSpec((1, HEAD_W), lambda b, h, q: (0, 0))],
        out_specs=[qspec] * 3,
        out_shape=[jax.ShapeDtypeStruct((batch, seq, SEC), BF16)] * 3,
        scratch_shapes=[pltpu.VMEM((seq, HEAD_W), BF16)] * 4 + [pltpu.VMEM((n_mem, HEAD_W), BF16)] * 2
        + [pltpu.VMEM((HEAD_W, HEAD_W), BF16)],
        compiler_params=pltpu.CompilerParams(dimension_semantics=("parallel", "parallel", "arbitrary")),
        name="attn_prompt",
    )(lam, r3(qa, seq), r3(qb, seq), r3(qm, seq), r3(ka, seq), r3(va, seq), r3(kb, seq), r3(vb, seq),
      r3(mk, n_mem), r3(mv, n_mem), gsub)
    return [o.reshape(batch * seq, SEC) for o in outs]


def _sample_scan_kernel(pt_ref, lam_ref, qa_ref, qb_ref, kbn_ref, vbn_ref, gsub_ref, *rest,
                        page_size, n_past_blocks):
    npg = PAGES_PER_STEP
    ka_pages, kb_pages, vb_pages = rest[:npg], rest[npg:2 * npg], rest[2 * npg:3 * npg]
    sa_o, sel_o, ob_o = rest[3 * npg:3 * npg + 3]
    m_s, l_s, acc_s, gate_s = rest[3 * npg + 3:]
    j = pl.program_id(1)
    rows = 2 * N_HEADS
    row = lax.broadcasted_iota(I32, (rows, SEC), 0)
    lane = lax.broadcasted_iota(I32, (rows, SEC), 1)
    lane128 = lax.broadcasted_iota(I32, (rows, HEAD_W), 1)
    qa_rows = jnp.where(lane // HEAD_W == row, qa_ref[...].astype(F32), 0.0)
    qb_rows = jnp.where(lane // DH_B == row, qb_ref[...].astype(F32), 0.0)
    qa16 = qa_rows.astype(BF16)
    qb16 = qb_rows.astype(BF16)
    scale_b = DH_B ** -0.5
    pages_per_block = MOBA_BLOCK // page_size

    @pl.when(j == 0)
    def _():
        m_s[...] = jnp.full(m_s.shape, NEG_INF, F32)
        l_s[...] = jnp.zeros(l_s.shape, F32)
        acc_s[...] = jnp.zeros(acc_s.shape, F32)
        gate_s[...] = jnp.zeros(gate_s.shape, F32)

    gate = gate_s[...]
    sb = []
    for p in range(npg):
        page = j * npg + p
        sa = _nt(qa16, ka_pages[p][...].astype(BF16))
        sa_o[page] = sa
        gate = gate + jnp.where(lane128 == page // pages_per_block, jnp.sum(sa, axis=-1, keepdims=True), 0.0)
        sb.append(_nt(qb16, kb_pages[p][...].astype(BF16)) * scale_b)
    gate_s[...] = gate
    sb = jnp.concatenate(sb, axis=1)
    m_old = m_s[:, 0:1]
    m_new = jnp.maximum(m_old, jnp.max(sb, axis=-1, keepdims=True))
    alpha = jnp.exp(m_old - m_new)
    pf = jnp.exp(sb - m_new)
    pb = pf.astype(BF16)
    pv = _dot(pb[:, 0:page_size], vb_pages[0][...].astype(BF16))
    for p in range(1, npg):
        pv = pv + _dot(pb[:, p * page_size:(p + 1) * page_size], vb_pages[p][...].astype(BF16))
    l_new = alpha * l_s[:, 0:1] + jnp.sum(pf, axis=-1, keepdims=True)
    acc = alpha * acc_s[...] + pv
    m_s[...] = jnp.broadcast_to(m_new, m_s.shape)
    l_s[...] = jnp.broadcast_to(l_new, l_s.shape)
    acc_s[...] = acc

    @pl.when(j == pl.num_programs(1) - 1)
    def _():
        s_new = jnp.sum(qb_rows * kbn_ref[...], axis=-1, keepdims=True) * scale_b
        m_f = jnp.maximum(m_new, s_new)
        a_f = jnp.exp(m_new - m_f)
        p_new = jnp.exp(s_new - m_f)
        o = (a_f * acc + p_new * vbn_ref[...]) / (a_f * l_new + p_new)
        coef = jnp.where(row % 2 == 0, 1.0, -lam_ref[0])
        ob = jnp.sum(jnp.where(lane // HEAD_W == row // 2, o * coef, 0.0), axis=0, keepdims=True)
        for hh in range(N_HEADS):
            sl = slice(hh * HEAD_W, (hh + 1) * HEAD_W)
            ob_o[:, sl] = _subln(ob[:, sl], gsub_ref[...]).astype(BF16)
        g = jnp.where(lane128 < n_past_blocks, gate * (1.0 / MOBA_BLOCK), NEG_INF)
        picks = jnp.zeros((rows, HEAD_W), I32)
        for t in range(MOBA_TOPK):
            mx = jnp.max(g, axis=-1, keepdims=True)
            idx = jnp.min(jnp.where(g == mx, lane128, HEAD_W), axis=-1, keepdims=True)
            picks = jnp.where(lane128 == t, idx, picks)
            g = jnp.where(lane128 == idx, NEG_INF, g)
        sel_o[...] = picks


def _sample_gather_kernel(page_ref, qa_ref, kan_ref, van_ref, sa_ref, va_hbm, oa_o, vbuf, sems, *, n, n_pages):
    i = pl.program_id(0)
    page_size = vbuf.shape[2]
    pages_per_block = MOBA_BLOCK // page_size
    n_sel = N_HEADS * MOBA_TOPK * pages_per_block
    slot = i % 2

    def v_copy(seq, c, s):
        hh = c // (MOBA_TOPK * pages_per_block)
        return pltpu.make_async_copy(va_hbm.at[page_ref[seq * 2 * n_sel + c], :, pl.ds(hh * HEAD_W, HEAD_W)],
                                     vbuf.at[s, c], sems.at[s])

    def start_pages(seq, s):
        for c in range(n_sel):
            v_copy(seq, c, s).start()

    @pl.when(i == 0)
    def _():
        start_pages(0, 0)

    if n > 1:
        @pl.when(i + 1 < n)
        def _():
            start_pages(i + 1, 1 - slot)

    for c in range(n_sel):
        v_copy(i, c, slot).wait()

    scale_a = HEAD_W ** -0.5
    qa = qa_ref[...].astype(F32)
    for hh in range(N_HEADS):
        sl = slice(hh * HEAD_W, (hh + 1) * HEAD_W)
        c0 = hh * MOBA_TOPK * pages_per_block
        s = jnp.concatenate([sa_ref[page_ref[i * 2 * n_sel + n_sel + c0 + c], hh:hh + 1, :]
                             for c in range(MOBA_TOPK * pages_per_block)], axis=1) * scale_a
        s_self = jnp.sum(qa[:, sl] * kan_ref[:, sl], axis=-1, keepdims=True) * scale_a
        m = jnp.maximum(jnp.max(s, axis=-1, keepdims=True), s_self)
        p = jnp.exp(s - m)
        p_self = jnp.exp(s_self - m)
        l = jnp.sum(p, axis=-1, keepdims=True) + p_self
        p16 = jnp.broadcast_to(p, (8, p.shape[1])).astype(BF16)
        acc = p_self * van_ref[:, sl]
        for c in range(MOBA_TOPK * pages_per_block):
            v = vbuf[slot, c0 + c].astype(BF16)
            acc = acc + _dot(p16[:, c * page_size:(c + 1) * page_size], v)[0:1, :]
        oa_o[:, sl] = (acc / l).astype(BF16)


def _attention_sample(lam, page_table, qa, qb, ka_new, va_new, kb_new, vb_new, gsub,
                      cache_ka, cache_va, cache_kb, cache_vb):
    ns, n_pages = page_table.shape
    n_pool, page_size = cache_ka.shape[0], cache_ka.shape[1]
    past = n_pages * page_size
    assert past % MOBA_BLOCK == 0 and MOBA_BLOCK % page_size == 0 and n_pages % PAGES_PER_STEP == 0
    n_past_blocks = past // MOBA_BLOCK
    assert MOBA_TOPK <= n_past_blocks <= HEAD_W
    pages_per_block = MOBA_BLOCK // page_size
    npg = PAGES_PER_STEP
    rows = 2 * N_HEADS
    pt_flat = page_table.reshape(-1)
    r3 = lambda a: a.reshape(ns, 1, SEC)
    tok = pl.BlockSpec((None, 1, SEC), lambda s, j, pt: (s, 0, 0))
    smem = pl.BlockSpec(memory_space=pltpu.SMEM)

    def page_spec(p):
        return pl.BlockSpec((None, page_size, SEC), lambda s, j, pt: (pt[s * n_pages + j * npg + p], 0, 0))

    pages = [page_spec(p) for p in range(npg)]
    scan = pl.pallas_call(
        functools.partial(_sample_scan_kernel, page_size=page_size, n_past_blocks=n_past_blocks),
        grid_spec=pltpu.PrefetchScalarGridSpec(
            num_scalar_prefetch=1, grid=(ns, n_pages // npg),
            in_specs=[smem, tok, tok, tok, tok, pl.BlockSpec((1, HEAD_W), lambda s, j, pt: (0, 0))] + pages * 3,
            out_specs=[pl.BlockSpec((None, n_pages, rows, page_size), lambda s, j, pt: (s, 0, 0, 0)),
                       pl.BlockSpec((None, rows, HEAD_W), lambda s, j, pt: (s, 0, 0)),
                       tok],
            scratch_shapes=[pltpu.VMEM((rows, HEAD_W), F32)] * 2 + [pltpu.VMEM((rows, SEC), F32),
                                                                    pltpu.VMEM((rows, HEAD_W), F32)]),
        out_shape=[jax.ShapeDtypeStruct((ns, n_pages, rows, page_size), F32),
                   jax.ShapeDtypeStruct((ns, rows, HEAD_W), I32),
                   jax.ShapeDtypeStruct((ns, 1, SEC), BF16)],
        compiler_params=pltpu.CompilerParams(dimension_semantics=("parallel", "arbitrary")),
        name="sample_scan",
    )
    sa_all, picks, ob = scan(pt_flat, lam, r3(qa), r3(qb), r3(kb_new), r3(vb_new), gsub,
                             *([cache_ka] * npg), *([cache_kb] * npg), *([cache_vb] * npg))
    sel_flat = picks[:, :N_HEADS, :MOBA_TOPK].reshape(-1)
    oa = _sample_gather(pt_flat, sel_flat, sa_all, qa, ka_new, va_new, cache_va, n_pages)
    return oa, ob.reshape(ns, SEC)


def _sample_gather(pt_flat, sel_flat, sa_all, qa, ka_new, va_new, cache_va, n_pages):
    ns = qa.shape[0]
    page_size = cache_va.shape[1]
    pages_per_block = MOBA_BLOCK // page_size
    rows = 2 * N_HEADS
    n_sel = N_HEADS * MOBA_TOPK * pages_per_block
    r3 = lambda a: a.reshape(ns, 1, SEC)
    tok1 = pl.BlockSpec((None, 1, SEC), lambda s, pg: (s, 0, 0))
    blocks = jnp.clip(sel_flat.reshape(ns, N_HEADS * MOBA_TOPK), 0, n_pages // pages_per_block - 1)
    pos = (blocks[:, :, None] * pages_per_block + jnp.arange(pages_per_block, dtype=I32)).reshape(ns, n_sel)
    pool = jnp.take_along_axis(pt_flat.reshape(ns, n_pages), pos, axis=1)
    page_tab = jnp.concatenate([pool, pos], axis=1).reshape(-1).astype(I32)
    gather = pl.pallas_call(
        functools.partial(_sample_gather_kernel, n=ns, n_pages=n_pages),
        grid_spec=pltpu.PrefetchScalarGridSpec(
            num_scalar_prefetch=1, grid=(ns,),
            in_specs=[tok1, tok1, tok1,
                      pl.BlockSpec((None, n_pages, rows, page_size), lambda s, pg: (s, 0, 0, 0)),
                      pl.BlockSpec(memory_space=pl.ANY)],
            out_specs=tok1,
            scratch_shapes=[pltpu.VMEM((2, n_sel, page_size, HEAD_W), F32), pltpu.SemaphoreType.DMA((2,))]),
        out_shape=jax.ShapeDtypeStruct((ns, 1, SEC), BF16),
        compiler_params=pltpu.CompilerParams(dimension_semantics=("arbitrary",)),
        name="sample_gather",
    )
    oa = gather(page_tab, r3(qa), r3(ka_new), r3(va_new), sa_all, cache_va)
    return oa.reshape(ns, SEC)


def _mem_sample_kernel(qm_ref, mk_ref, mv_ref, om_o):
    rows = 2 * N_HEADS
    row = lax.broadcasted_iota(I32, (rows, SEC), 0)
    head = lax.broadcasted_iota(I32, (rows, SEC), 1) // HEAD_W
    q16 = jnp.where(head == row, qm_ref[...].astype(F32), 0.0).astype(BF16)
    s = _nt(q16, mk_ref[...].astype(BF16)) * (HEAD_W ** -0.5)
    _, l, acc = _softmax_start(s, mv_ref[...].astype(BF16))
    om_o[...] = jnp.sum(jnp.where(head == row, acc / l, 0.0), axis=0, keepdims=True).astype(BF16)


def _attention_mem_sample(qm, mk, mv):
    ns, n_mem, _ = mk.shape
    tok = pl.BlockSpec((None, 1, SEC), lambda s: (s, 0, 0))
    kv = pl.BlockSpec((None, n_mem, SEC), lambda s: (s, 0, 0))
    return pl.pallas_call(
        _mem_sample_kernel, grid=(ns,), in_specs=[tok, kv, kv], out_specs=tok,
        out_shape=jax.ShapeDtypeStruct((ns, 1, SEC), BF16),
        compiler_params=pltpu.CompilerParams(dimension_semantics=("parallel",)),
        name="mem_sample",
    )(qm.reshape(ns, 1, SEC), mk, mv).reshape(ns, SEC)


def _merge_kernel(*refs, n_main, n_steps):
    i = pl.program_id(0)
    hn_o = refs[-3]

    @pl.when(i < n_main)
    def _():
        _merge_tile(*refs)

    if n_main < n_steps:
        @pl.when(i >= n_main)
        def _():
            hn_o[...] = jnp.zeros(hn_o.shape, hn_o.dtype)


def _merge_tile(x_ref, oa_ref, ob_ref, om_ref, g_ref, wa_ref, wb_ref, wm_ref, wo_ref, gffn_ref,
                wrh_ref, wrl_ref, br_ref, *rest):
    xmid_o, hn_o, idx_o, gate_o = rest[-4:]
    d = x_ref.shape[1]
    n_exp = wrh_ref.shape[0]
    merged = (g_ref[:, 0:d].astype(F32) * _dot(oa_ref[...], wa_ref[...])
              + g_ref[:, d:2 * d].astype(F32) * _dot(ob_ref[...], wb_ref[...])
              + g_ref[:, 2 * d:3 * d].astype(F32) * _dot(om_ref[...], wm_ref[...]))
    xm = x_ref[...] + _dot(merged.astype(BF16), wo_ref[...])
    xmid_o[...] = xm
    hn = _rms(xm, RMS_EPS) * gffn_ref[...]
    hn_o[...] = hn
    hi = hn.astype(BF16)
    lo = (hn - hi.astype(F32)).astype(BF16)
    g = _nt(wrh_ref[...], hi) + _nt(wrl_ref[...], hi) + _nt(wrh_ref[...], lo) + br_ref[...]
    erow = lax.broadcasted_iota(I32, g.shape, 0)
    vals, idxs = [], []
    for _ in range(MOE_TOPK):
        mx = jnp.max(g, axis=0, keepdims=True)
        ix = jnp.min(jnp.where(g == mx, erow, n_exp), axis=0, keepdims=True)
        vals.append(mx)
        idxs.append(ix)
        g = jnp.where(erow == ix, NEG_INF, g)
    e = jnp.exp(jnp.concatenate(vals, axis=0) - vals[0])
    gate_o[...] = e / jnp.sum(e, axis=0, keepdims=True)
    idx_o[...] = jnp.concatenate(idxs, axis=0)


def _merge(x2d, oa, ob, om, g, tm, w_a, w_b, w_m, w_o, gffn, wr_hi, wr_lo, br, hn_all, row_block0):
    n, d = x2d.shape
    n_exp = wr_hi.shape[0]
    n_main = n // tm
    creates = isinstance(hn_all, tuple)
    n_all = hn_all[0] if creates else hn_all.shape[0]
    n_steps = pl.cdiv(n_all, tm) if creates else n_main
    last = n_main - 1
    row = lambda w: pl.BlockSpec((tm, w), lambda i: (jnp.minimum(i, last), 0))
    col = pl.BlockSpec((MOE_TOPK, tm), lambda i: (0, jnp.minimum(i, last)))
    in_specs = [row(d), row(SEC), row(SEC), row(SEC), row(3 * d),
                _const_spec((SEC, d)), _const_spec((SEC, d)), _const_spec((SEC, d)), _const_spec((d, d)),
                _const_spec((1, d)), _const_spec((n_exp, d)), _const_spec((n_exp, d)), _const_spec((n_exp, 1))]
    args = [x2d, oa, ob, om, g, w_a, w_b, w_m, w_o, gffn, wr_hi, wr_lo, br]
    aliases = {}
    if not creates:
        in_specs.append(pl.BlockSpec(memory_space=pl.ANY))
        args.append(hn_all)
        aliases = {len(args) - 1: 1}
    return pl.pallas_call(
        functools.partial(_merge_kernel, n_main=n_main, n_steps=n_steps), grid=(n_steps,), in_specs=in_specs,
        out_specs=[row(d), pl.BlockSpec((tm, d), lambda i: (row_block0 + i, 0)), col, col],
        out_shape=[jax.ShapeDtypeStruct((n, d), F32), jax.ShapeDtypeStruct((n_all, d), F32),
                   jax.ShapeDtypeStruct((MOE_TOPK, n), I32), jax.ShapeDtypeStruct((MOE_TOPK, n), F32)],
        input_output_aliases=aliases,
        compiler_params=pltpu.CompilerParams(dimension_semantics=("arbitrary",), vmem_limit_bytes=VMEM_LIMIT),
        name="merge",
    )(*args)


def _row_copy(src_hbm, idx_ref, r, dst, sem):
    return pltpu.make_async_copy(src_hbm.at[pl.ds(idx_ref[0, r], 1), :], dst.at[pl.ds(r, 1), :], sem)


def _expert_kernel(be_ref, src_ref, nxt_ref, gate_ref, hn_hbm, w1_ref, b1_ref, w2_ref, b2_ref, y_o, xbuf, sems,
                   *, n):
    i = pl.program_id(0)
    nrow = xbuf.shape[1]
    d_ff = w2_ref.shape[0]
    slot = i % 2

    def start_rows(idx_ref, s):
        def body(r, c):
            _row_copy(hn_hbm, idx_ref, r, xbuf.at[s], sems.at[s]).start()
            return c
        lax.fori_loop(0, nrow, body, 0)

    @pl.when(i == 0)
    def _():
        start_rows(src_ref, 0)

    if n > 1:
        @pl.when(i + 1 < n)
        def _():
            start_rows(nxt_ref, 1 - slot)

    pltpu.make_async_copy(hn_hbm.at[pl.ds(0, nrow), :], xbuf.at[slot], sems.at[slot]).wait()
    x = xbuf[slot].astype(BF16)
    u = _dot(x, w1_ref[...]) + b1_ref[...]
    u_glu = jnp.minimum(u[:, :d_ff], SWIGLU_LIMIT)
    u_lin = jnp.clip(u[:, d_ff:], -SWIGLU_LIMIT, SWIGLU_LIMIT)
    act = u_glu * (1.0 / (1.0 + jnp.exp(-SWIGLU_ALPHA * u_glu))) * (u_lin + 1.0)
    y = _dot(act.astype(BF16), w2_ref[...]) + b2_ref[...]
    eye = lax.broadcasted_iota(I32, (nrow, nrow), 0) == lax.broadcasted_iota(I32, (nrow, nrow), 1)
    gcol = jnp.sum(jnp.where(eye, gate_ref[...], 0.0), axis=-1, keepdims=True)
    y_o[...] = y * gcol


def _experts(blk_expert, row_src, row_gate, hn_all, w1, b1, w2, b2):
    n_blocks = blk_expert.shape[0]
    n_exp, d, f2 = w1.shape
    d_ff = f2 // 2
    src3 = row_src.reshape(n_blocks, 1, MOE_BLOCK)
    gate3 = row_gate.reshape(n_blocks, 1, MOE_BLOCK)
    idx_spec = lambda off: pl.BlockSpec((None, 1, MOE_BLOCK),
                                        lambda i, be: (jnp.minimum(i + off, n_blocks - 1), 0, 0),
                                        memory_space=pltpu.SMEM)
    return pl.pallas_call(
        functools.partial(_expert_kernel, n=n_blocks),
        grid_spec=pltpu.PrefetchScalarGridSpec(
            num_scalar_prefetch=1, grid=(n_blocks,),
            in_specs=[idx_spec(0), idx_spec(1),
                      pl.BlockSpec((None, 1, MOE_BLOCK), lambda i, be: (i, 0, 0)),
                      pl.BlockSpec(memory_space=pl.ANY),
                      pl.BlockSpec((None, d, f2), lambda i, be: (be[i], 0, 0)),
                      pl.BlockSpec((None, 1, f2), lambda i, be: (be[i], 0, 0)),
                      pl.BlockSpec((None, d_ff, d), lambda i, be: (be[i], 0, 0)),
                      pl.BlockSpec((None, 1, d), lambda i, be: (be[i], 0, 0))],
            out_specs=pl.BlockSpec((MOE_BLOCK, d), lambda i, be: (i, 0)),
            scratch_shapes=[pltpu.VMEM((2, MOE_BLOCK, d), F32), pltpu.SemaphoreType.DMA((2,))]),
        out_shape=jax.ShapeDtypeStruct((n_blocks * MOE_BLOCK, d), F32),
        compiler_params=pltpu.CompilerParams(dimension_semantics=("arbitrary",), vmem_limit_bytes=VMEM_LIMIT),
        name="experts",
    )(blk_expert, src3, src3, gate3, hn_all, w1, b1.reshape(n_exp, 1, f2), w2, b2.reshape(n_exp, 1, d))


def _combine_kernel(pos_ref, nxt_ref, xmid_ref, ys_hbm, y_o, buf, sems, *, n):
    i = pl.program_id(0)
    tm = xmid_ref.shape[0]
    nrow = MOE_TOPK * tm
    slot = i % 2

    def start_rows(idx_ref, s):
        def body(r, c):
            _row_copy(ys_hbm, idx_ref, r, buf.at[s], sems.at[s]).start()
            return c
        lax.fori_loop(0, nrow, body, 0)

    @pl.when(i == 0)
    def _():
        start_rows(pos_ref, 0)

    if n > 1:
        @pl.when(i + 1 < n)
        def _():
            start_rows(nxt_ref, 1 - slot)

    pltpu.make_async_copy(ys_hbm.at[pl.ds(0, nrow), :], buf.at[slot], sems.at[slot]).wait()
    acc = xmid_ref[...]
    for k in range(MOE_TOPK):
        acc = acc + buf[slot, k * tm:(k + 1) * tm, :]
    y_o[...] = acc


def _combine(pos_kt, xmid, ys, tm):
    n, d = xmid.shape
    n_tiles = n // tm
    pos3 = pos_kt.reshape(MOE_TOPK, n_tiles, tm).transpose(1, 0, 2).reshape(n_tiles, 1, MOE_TOPK * tm)
    idx_spec = lambda off: pl.BlockSpec((None, 1, MOE_TOPK * tm),
                                        lambda i: (jnp.minimum(i + off, n_tiles - 1), 0, 0),
                                        memory_space=pltpu.SMEM)
    return pl.pallas_call(
        functools.partial(_combine_kernel, n=n_tiles), grid=(n_tiles,),
        in_specs=[idx_spec(0), idx_spec(1), pl.BlockSpec((tm, d), lambda i: (i, 0)),
                  pl.BlockSpec(memory_space=pl.ANY)],
        out_specs=pl.BlockSpec((tm, d), lambda i: (i, 0)),
        out_shape=jax.ShapeDtypeStruct((n, d), F32),
        scratch_shapes=[pltpu.VMEM((2, MOE_TOPK * tm, d), F32), pltpu.SemaphoreType.DMA((2,))],
        compiler_params=pltpu.CompilerParams(dimension_semantics=("arbitrary",)),
        name="combine",
    )(pos3, pos3, xmid, ys)


def _route(e_tk, g_tk, n_exp):
    n_tok = e_tk.shape[0]
    n_as = n_tok * MOE_TOPK
    e_flat = e_tk.reshape(n_as)
    g_flat = g_tk.reshape(n_as)
    tok_flat = jnp.repeat(jnp.arange(n_tok, dtype=I32), MOE_TOPK)
    order = jnp.argsort(e_flat, stable=True)
    e_sorted = e_flat[order]
    counts = jnp.bincount(e_flat, length=n_exp)
    padded = (counts + MOE_BLOCK - 1) // MOE_BLOCK * MOE_BLOCK
    pad_end = jnp.cumsum(padded)
    pad_start = pad_end - padded
    start = jnp.cumsum(counts) - counts
    dest = (pad_start[e_sorted] + jnp.arange(n_as, dtype=I32) - start[e_sorted]).astype(I32)
    n_blocks = -(-n_as // MOE_BLOCK) + n_exp
    n_rows = n_blocks * MOE_BLOCK
    row_src = jnp.zeros((n_rows,), I32).at[dest].set(tok_flat[order])
    row_gate = jnp.zeros((n_rows,), F32).at[dest].set(g_flat[order])
    blk_expert = jnp.minimum(
        jnp.searchsorted(pad_end, jnp.arange(n_blocks, dtype=I32) * MOE_BLOCK, side="right"), n_exp - 1).astype(I32)
    pos = jnp.zeros((n_as,), I32).at[order].set(dest)
    return blk_expert, row_src, row_gate, pos.reshape(n_tok, MOE_TOPK).T


def kernel(x_prompt, x_sample, mem_prompt, cache_moba_k, cache_moba_v, cache_diff_k, cache_diff_v, cache_mem_k, cache_mem_v, page_table, w_in, b_gate, g_norm_mix, g_qa, g_ka, g_qb, g_kb, lambda_q1, lambda_k1, lambda_q2, lambda_k2, g_subln, g_norm_mem, w_mem_kv, g_qm, g_km, w_br_a, w_br_b, w_br_m, w_out, g_norm_ffn, w_router, b_router, w1, b1, w2, b2):
    assert w_in.shape[0] == 1, "single-layer trunk"
    batch, seq, d = x_prompt.shape
    ns, dec_seq, _ = x_sample.shape
    assert dec_seq == 1 and seq % MOBA_BLOCK == 0 and seq // MOBA_BLOCK >= MOBA_TOPK
    n_mem = mem_prompt.shape[1]
    n_pages, page_size = page_table.shape[1], cache_moba_k.shape[2]
    n_pool = cache_moba_k.shape[1]
    np_tok = batch * seq
    n_exp = w_router.shape[-1]
    row1 = lambda a: a.reshape(1, -1).astype(F32)

    lam = (jnp.exp(jnp.sum(lambda_q1[0].astype(F32) * lambda_k1[0].astype(F32)))
           - jnp.exp(jnp.sum(lambda_q2[0].astype(F32) * lambda_k2[0].astype(F32))) + LAM_INIT).reshape(1)
    w_in16 = w_in[0].astype(BF16)
    gqb2, gkb2 = (row1(jnp.tile(g[0], HEAD_W // DH_B)) for g in (g_qb, g_kb))
    proj_w = (w_in16, row1(g_norm_mix), row1(b_gate), row1(g_qa), row1(g_ka), gqb2, gkb2, row1(g_qm))
    gsub = row1(g_subln)

    tm_p = 512 if seq % 512 == 0 else MOBA_BLOCK
    pos_p = jnp.arange(seq, dtype=I32)
    qa, ka, va, qb, kb, vb, qm, gates = _project(
        x_prompt.reshape(np_tok, d), tm_p, seq // tm_p, *proj_w, _rope_tables(pos_p, HEAD_W), _rope_tables(pos_p, DH_B))
    mk, mv = _memory_kv(mem_prompt.reshape(batch * n_mem, d), min(512, batch * n_mem), w_mem_kv[0].astype(BF16),
                        row1(g_norm_mem), row1(g_km))
    oa, ob, om = _attention_prompt(lam, qa, qb, qm, ka, va, kb, vb, mk, mv, gsub, batch, seq, n_mem)

    pos_s = jnp.full((ns,), n_pages * page_size, dtype=I32)
    qa_s, ka_s, va_s, qb_s, kb_s, vb_s, qm_s, gates_s = _project(
        x_sample.reshape(ns, d), ns, 1, *proj_w, _rope_tables(pos_s, HEAD_W), _rope_tables(pos_s, DH_B))
    page_rows = lambda c: c.reshape(n_pool, page_size, SEC)
    oa_s, ob_s = _attention_sample(lam, page_table, qa_s, qb_s, ka_s, va_s, kb_s, vb_s, gsub,
                                   page_rows(cache_moba_k), page_rows(cache_moba_v),
                                   page_rows(cache_diff_k), page_rows(cache_diff_v))
    om_s = _attention_mem_sample(qm_s, cache_mem_k.reshape(ns, n_mem, SEC), cache_mem_v.reshape(ns, n_mem, SEC))

    wr = w_router[0].T.astype(F32)
    wr_hi = wr.astype(BF16)
    wr_lo = (wr - wr_hi.astype(F32)).astype(BF16)
    merge_w = (w_br_a[0].astype(BF16), w_br_b[0].astype(BF16), w_br_m[0].astype(BF16), w_out[0].astype(BF16),
               row1(g_norm_ffn), wr_hi, wr_lo, b_router[0].reshape(n_exp, 1).astype(F32))
    n_all = np_tok + ns
    xmid_p, hn_all, e_p, g_p = _merge(x_prompt.reshape(np_tok, d), oa, ob, om, gates, tm_p, *merge_w, (n_all,), 0)
    xmid_s, hn_all, e_s, g_s = _merge(x_sample.reshape(ns, d), oa_s, ob_s, om_s, gates_s, ns, *merge_w, hn_all,
                                      np_tok // ns)

    e_tk = jnp.concatenate([e_p, e_s], axis=1).T
    g_tk = jnp.concatenate([g_p, g_s], axis=1).T
    blk_expert, row_src, row_gate, pos_kt = _route(e_tk, g_tk, n_exp)
    ys = _experts(blk_expert, row_src, row_gate, hn_all, w1[0].astype(BF16), b1[0].astype(F32),
                  w2[0].astype(BF16), b2[0].astype(F32))
    y_p = _combine(pos_kt[:, :np_tok], xmid_p, ys, 128)
    y_s = _combine(pos_kt[:, np_tok:], xmid_s, ys, ns)

    k5 = lambda a, n: a.reshape(1, n, -1, N_HEADS, HEAD_W)
    k6 = lambda a, n: a.reshape(1, n, -1, N_HEADS, 2, DH_B)
    return (y_p.reshape(batch, seq, d), y_s.reshape(ns, 1, d),
            k5(ka, batch), k5(va, batch), k6(kb, batch), k5(vb, batch),
            k5(mk, batch), k5(mv, batch),
            k5(ka_s, ns), k5(va_s, ns), k6(kb_s, ns), k5(vb_s, ns))
```

```python
import functools
import math

import jax
import jax.numpy as jnp
from jax import lax
from jax.experimental import pallas as pl
from jax.experimental.pallas import tpu as pltpu

F32 = jnp.float32
BF16 = jnp.bfloat16
I32 = jnp.int32

N_HEADS = 4
HEAD_W = 128
SEC = N_HEADS * HEAD_W
DH_B = 64
MOBA_BLOCK = 256
MOBA_TOPK = 3
ROPE_THETA = 500000.0
ROT_FRACTION = 4
RMS_EPS = 1e-6
SUBLN_EPS = 1e-5
LAM_INIT = 0.8 - 0.6 * math.exp(-0.3 * 0)
MOE_TOPK = 4
MOE_BLOCK = 256
SWIGLU_ALPHA = 1.702
SWIGLU_LIMIT = 7.0
PAGES_PER_STEP = 8
NEG_INF = float("-inf")
VMEM_LIMIT = 56 * 1024 * 1024

_NT = (((1,), (1,)), ((), ()))


def _nt(a, b):
    return lax.dot_general(a, b, _NT, preferred_element_type=F32)


def _dot(a, b):
    return jnp.dot(a, b, preferred_element_type=F32)


def _rms(x, eps):
    return x * lax.rsqrt(jnp.mean(x * x, axis=-1, keepdims=True) + eps)


def _headnorm128(z, g):
    return _rms(z, RMS_EPS) * g


def _headnorm64(z, g):
    zz = z * z
    lo = lax.broadcasted_iota(I32, z.shape, 1) < DH_B
    tot = jnp.sum(zz, axis=-1, keepdims=True)
    s_lo = jnp.sum(jnp.where(lo, zz, 0.0), axis=-1, keepdims=True)
    s_hi = tot - s_lo
    r = jnp.where(lo, lax.rsqrt(s_lo * (1.0 / DH_B) + RMS_EPS), lax.rsqrt(s_hi * (1.0 / DH_B) + RMS_EPS))
    return z * r * g


def _rope(y, c, s_up, s_dn, half):
    return y * c + pltpu.roll(y, HEAD_W - half, 1) * s_up + pltpu.roll(y, half, 1) * s_dn


def _rope_tables(pos, d):
    rot = d // ROT_FRACTION
    half = rot // 2
    inv = ROPE_THETA ** (-(jnp.arange(half, dtype=F32) * 2.0 / rot))
    ang = pos.astype(F32)[:, None] * inv[None, :]
    cos, sin = jnp.cos(ang), jnp.sin(ang)
    p = pos.shape[0]
    c = jnp.concatenate([cos, cos, jnp.ones((p, d - rot), F32)], axis=-1)
    s_up = jnp.concatenate([-sin, jnp.zeros((p, d - half), F32)], axis=-1)
    s_dn = jnp.concatenate([jnp.zeros((p, half), F32), sin, jnp.zeros((p, d - rot), F32)], axis=-1)
    rep = HEAD_W // d
    return tuple(jnp.tile(t, (1, rep)) for t in (c, s_up, s_dn))


def _proj_kernel(x_ref, w_ref, gmix_ref, bg_ref, gqa_ref, gka_ref, gqb_ref, gkb_ref, gqm_ref,
                 ca_ref, sua_ref, sda_ref, cb_ref, sub_ref, sdb_ref,
                 qa_o, ka_o, va_o, qb_o, kb_o, vb_o, qm_o, g_o):
    h = (_rms(x_ref[...], RMS_EPS) * gmix_ref[...]).astype(BF16)

    def sec(k):
        return _dot(h, w_ref[:, k * SEC:(k + 1) * SEC])

    def per_head(z, fn, out):
        for hh in range(N_HEADS):
            sl = slice(hh * HEAD_W, (hh + 1) * HEAD_W)
            out[:, sl] = fn(z[:, sl]).astype(out.dtype)

    def rope_a(y):
        return _rope(y, ca_ref[...], sua_ref[...], sda_ref[...], HEAD_W // ROT_FRACTION // 2)

    def rope_b(y):
        return _rope(y, cb_ref[...], sub_ref[...], sdb_ref[...], DH_B // ROT_FRACTION // 2)

    per_head(sec(0), lambda z: rope_a(_headnorm128(z, gqa_ref[...])), qa_o)
    per_head(sec(1), lambda z: rope_a(_headnorm128(z, gka_ref[...])), ka_o)
    va_o[...] = sec(2)
    per_head(sec(3), lambda z: rope_b(_headnorm64(z, gqb_ref[...])), qb_o)
    per_head(sec(4), lambda z: rope_b(_headnorm64(z, gkb_ref[...])), kb_o)
    vb_o[...] = sec(5)
    per_head(sec(6), lambda z: _headnorm128(z, gqm_ref[...]), qm_o)
    n_gate_sec = g_o.shape[1] // SEC
    for k in range(n_gate_sec):
        z = sec(7 + k) + bg_ref[:, k * SEC:(k + 1) * SEC]
        g_o[:, k * SEC:(k + 1) * SEC] = (1.0 / (1.0 + jnp.exp(-z))).astype(BF16)


def _const_spec(shape):
    return pl.BlockSpec(shape, lambda *_: (0,) * len(shape), pipeline_mode=pl.Buffered(1))


def _project(x2d, tm, n_pos_blocks, w16, gmix, bgate, gqa, gka, gqb2, gkb2, gqm, tabs_a, tabs_b):
    n, d = x2d.shape
    ncol = w16.shape[1]
    n_gate = ncol - 7 * SEC
    row = lambda w: pl.BlockSpec((tm, w), lambda i: (i, 0))
    tab = pl.BlockSpec((tm, HEAD_W), lambda i: (i % n_pos_blocks, 0))
    in_specs = [row(d), _const_spec((d, ncol)), _const_spec((1, d)), _const_spec((1, n_gate))]
    in_specs += [_const_spec((1, HEAD_W))] * 5 + [tab] * 6
    out_shape = [jax.ShapeDtypeStruct((n, SEC), dt) for dt in (BF16, F32, F32, BF16, F32, F32, BF16)]
    out_shape.append(jax.ShapeDtypeStruct((n, n_gate), BF16))
    out_specs = [row(SEC)] * 7 + [row(n_gate)]
    return pl.pallas_call(
        _proj_kernel, grid=(n // tm,), in_specs=in_specs, out_specs=out_specs, out_shape=out_shape,
        compiler_params=pltpu.CompilerParams(dimension_semantics=("parallel",), vmem_limit_bytes=VMEM_LIMIT),
        name="proj",
    )(x2d, w16, gmix, bgate, gqa, gka, gqb2, gkb2, gqm, *tabs_a, *tabs_b)


def _memkv_kernel(x_ref, w_ref, gn_ref, gkm_ref, k_o, v_o):
    h = (_rms(x_ref[...], RMS_EPS) * gn_ref[...]).astype(BF16)
    zk = _dot(h, w_ref[:, :SEC])
    for hh in range(N_HEADS):
        sl = slice(hh * HEAD_W, (hh + 1) * HEAD_W)
        k_o[:, sl] = _headnorm128(zk[:, sl], gkm_ref[...])
    v_o[...] = _dot(h, w_ref[:, SEC:])


def _memory_kv(mem2d, tm, w16, gn, gkm):
    n, d = mem2d.shape
    row = lambda w: pl.BlockSpec((tm, w), lambda i: (i, 0))
    return pl.pallas_call(
        _memkv_kernel, grid=(n // tm,),
        in_specs=[row(d), _const_spec((d, 2 * SEC)), _const_spec((1, d)), _const_spec((1, HEAD_W))],
        out_specs=[row(SEC), row(SEC)],
        out_shape=[jax.ShapeDtypeStruct((n, SEC), F32)] * 2,
        compiler_params=pltpu.CompilerParams(dimension_semantics=("parallel",)),
        name="memkv",
    )(mem2d, w16, gn, gkm)


def _softmax_start(s, v16):
    m = jnp.max(s, axis=-1, keepdims=True)
    p = jnp.exp(s - m)
    return m, jnp.sum(p, axis=-1, keepdims=True), _dot(p.astype(BF16), v16)


def _softmax_update(s, v16, m, l, acc):
    m_new = jnp.maximum(m, jnp.max(s, axis=-1, keepdims=True))
    alpha = jnp.exp(m - m_new)
    p = jnp.exp(s - m_new)
    return m_new, alpha * l + jnp.sum(p, axis=-1, keepdims=True), alpha * acc + _dot(p.astype(BF16), v16)


def _subln(o, g):
    return _rms(o, SUBLN_EPS) * g * (1.0 - LAM_INIT)


def _attn_prompt_kernel(lam_ref, qa_ref, qb_ref, qm_ref, ka_ref, va_ref, kb_ref, vb_ref, mk_ref, mv_ref,
                        gsub_ref, oa_o, ob_o, om_o, ka16, va16, kb16, vb16, mk16, mv16, kmean16):
    qi = pl.program_id(2)
    seq = ka_ref.shape[0]
    tq = qa_ref.shape[0]
    nb = seq // MOBA_BLOCK
    lam = lam_ref[0]

    @pl.when(qi == 0)
    def _():
        ka = ka_ref[...]
        ka16[...] = ka.astype(BF16)
        va16[...] = va_ref[...].astype(BF16)
        kb16[...] = kb_ref[...].astype(BF16)
        vb16[...] = vb_ref[...].astype(BF16)
        mk16[...] = mk_ref[...].astype(BF16)
        mv16[...] = mv_ref[...].astype(BF16)
        km = jnp.mean(ka.reshape(nb, MOBA_BLOCK, HEAD_W), axis=1)
        kmean16[...] = jnp.concatenate([km, jnp.zeros((HEAD_W - nb, HEAD_W), F32)], axis=0).astype(BF16)

    qa = qa_ref[...]
    qb = qb_ref[...]
    lane = lax.broadcasted_iota(I32, (tq, HEAD_W), 1)
    scale_a = HEAD_W ** -0.5
    scale_b = DH_B ** -0.5

    g = jnp.where(lane < qi, _nt(qa, kmean16[...]), NEG_INF)
    sel = jnp.zeros((tq, HEAD_W), F32)
    for _ in range(MOBA_TOPK):
        mx = jnp.max(g, axis=-1, keepdims=True)
        idx = jnp.min(jnp.where(g == mx, lane, HEAD_W), axis=-1, keepdims=True)
        pick = (lane == idx) & (mx > NEG_INF)
        sel = jnp.where(pick, 1.0, sel)
        g = jnp.where(pick, NEG_INF, g)

    qbf = qb.astype(F32)
    qb0 = jnp.where(lane < DH_B, qbf, 0.0).astype(BF16)
    qb1 = jnp.where(lane >= DH_B, qbf, 0.0).astype(BF16)

    causal = lax.broadcasted_iota(I32, (tq, tq), 1) <= lax.broadcasted_iota(I32, (tq, tq), 0)
    d0 = pl.multiple_of(qi * tq, tq)
    kb_d = kb16[pl.ds(d0, tq), :]
    vb_d = vb16[pl.ds(d0, tq), :]
    st_a = _softmax_start(jnp.where(causal, _nt(qa, ka16[pl.ds(d0, tq), :]) * scale_a, NEG_INF),
                          va16[pl.ds(d0, tq), :])
    st_0 = _softmax_start(jnp.where(causal, _nt(qb0, kb_d) * scale_b, NEG_INF), vb_d)
    st_1 = _softmax_start(jnp.where(causal, _nt(qb1, kb_d) * scale_b, NEG_INF), vb_d)

    def past_block(n, carry):
        st_a, st_0, st_1 = carry
        off = pl.multiple_of(n * tq, tq)
        chosen = jnp.sum(jnp.where(lane == n, sel, 0.0), axis=-1, keepdims=True) > 0.5
        sa = jnp.where(chosen, _nt(qa, ka16[pl.ds(off, tq), :]) * scale_a, NEG_INF)
        st_a = _softmax_update(sa, va16[pl.ds(off, tq), :], *st_a)
        kb_n = kb16[pl.ds(off, tq), :]
        vb_n = vb16[pl.ds(off, tq), :]
        st_0 = _softmax_update(_nt(qb0, kb_n) * scale_b, vb_n, *st_0)
        st_1 = _softmax_update(_nt(qb1, kb_n) * scale_b, vb_n, *st_1)
        return st_a, st_0, st_1

    st_a, st_0, st_1 = lax.fori_loop(0, qi, past_block, (st_a, st_0, st_1))

    oa_o[...] = (st_a[2] / st_a[1]).astype(BF16)
    ob = st_0[2] / st_0[1] - lam * (st_1[2] / st_1[1])
    ob_o[...] = _subln(ob, gsub_ref[...]).astype(BF16)

    sm = _nt(qm_ref[...], mk16[...]) * scale_a
    _, lm, accm = _softmax_start(sm, mv16[...])
    om_o[...] = (accm / lm).astype(BF16)


def _attention_prompt(lam, qa, qb, qm, ka, va, kb, vb, mk, mv, gsub, batch, seq, n_mem):
    r3 = lambda a, rows: a.reshape(batch, rows, SEC)
    tq = MOBA_BLOCK
    qspec = pl.BlockSpec((None, tq, HEAD_W), lambda b, h, q: (b, q, h))
    kspec = pl.BlockSpec((None, seq, HEAD_W), lambda b, h, q: (b, 0, h))
    mspec = pl.BlockSpec((None, n_mem, HEAD_W), lambda b, h, q: (b, 0, h))
    outs = pl.pallas_call(
        _attn_prompt_kernel, grid=(batch, N_HEADS, seq // tq),
        in_specs=[pl.BlockSpec(memory_space=pltpu.SMEM), qspec, qspec, qspec, kspec, kspec, kspec, kspec,
                  mspec, mspec, pl.BlockSpec((1, HEAD_W), lambda b, h, q: (0, 0))],
        out_specs=[qspec] * 3,
        out_shape=[jax.ShapeDtypeStruct((batch, seq, SEC), BF16)] * 3,
        scratch_shapes=[pltpu.VMEM((seq, HEAD_W), BF16)] * 4 + [pltpu.VMEM((n_mem, HEAD_W), BF16)] * 2
        + [pltpu.VMEM((HEAD_W, HEAD_W), BF16)],
        compiler_params=pltpu.CompilerParams(dimension_semantics=("parallel", "parallel", "arbitrary")),
        name="attn_prompt",
    )(lam, r3(qa, seq), r3(qb, seq), r3(qm, seq), r3(ka, seq), r3(va, seq), r3(kb, seq), r3(vb, seq),
      r3(mk, n_mem), r3(mv, n_mem), gsub)
    return [o.reshape(batch * seq, SEC) for o in outs]


def _sample_scan_kernel(pt_ref, lam_ref, qa_ref, qb_ref, kbn_ref, vbn_ref, gsub_ref, *rest,
                        page_size, n_past_blocks):
    npg = PAGES_PER_STEP
    ka_pages, kb_pages, vb_pages = rest[:npg], rest[npg:2 * npg], rest[2 * npg:3 * npg]
    sa_o, sel_o, ob_o = rest[3 * npg:3 * npg + 3]
    m_s, l_s, acc_s, gate_s = rest[3 * npg + 3:]
    j = pl.program_id(1)
    rows = 2 * N_HEADS
    row = lax.broadcasted_iota(I32, (rows, SEC), 0)
    lane = lax.broadcasted_iota(I32, (rows, SEC), 1)
    lane128 = lax.broadcasted_iota(I32, (rows, HEAD_W), 1)
    qa = qa_ref[...].astype(F32)
    qa16 = [jnp.broadcast_to(qa[:, hh * HEAD_W:(hh + 1) * HEAD_W], (rows, HEAD_W)).astype(BF16)
            for hh in range(N_HEADS)]
    qb_rows = jnp.where(lane // DH_B == row, qb_ref[...].astype(F32), 0.0)
    qb16 = qb_rows.astype(BF16)
    row128 = lax.broadcasted_iota(I32, (rows, HEAD_W), 0)

    def head_rows(page_ref, hh):
        return page_ref[pl.ds(hh, page_size, stride=N_HEADS), :].astype(BF16)
    scale_b = DH_B ** -0.5
    pages_per_block = MOBA_BLOCK // page_size

    @pl.when(j == 0)
    def _():
        m_s[...] = jnp.full(m_s.shape, NEG_INF, F32)
        l_s[...] = jnp.zeros(l_s.shape, F32)
        acc_s[...] = jnp.zeros(acc_s.shape, F32)
        gate_s[...] = jnp.zeros(gate_s.shape, F32)

    gate = gate_s[...]
    sb = []
    for p in range(npg):
        page = j * npg + p
        sa = jnp.zeros((rows, page_size), F32)
        for hh in range(N_HEADS):
            sa = jnp.where(row128 == hh, _nt(qa16[hh], head_rows(ka_pages[p], hh)), sa)
        sa_o[page] = sa
        gate = gate + jnp.where(lane128 == page // pages_per_block, jnp.sum(sa, axis=-1, keepdims=True), 0.0)
        sb.append(_dot(qb16, kb_pages[p][...].astype(BF16)) * scale_b)
    gate_s[...] = gate
    sb = jnp.concatenate(sb, axis=1)
    m_old = m_s[:, 0:1]
    m_new = jnp.maximum(m_old, jnp.max(sb, axis=-1, keepdims=True))
    alpha = jnp.exp(m_old - m_new)
    pf = jnp.exp(sb - m_new)
    pb = pf.astype(BF16)
    pv = []
    for hh in range(N_HEADS):
        pv_h = _dot(pb[:, 0:page_size], head_rows(vb_pages[0], hh))
        for p in range(1, npg):
            pv_h = pv_h + _dot(pb[:, p * page_size:(p + 1) * page_size], head_rows(vb_pages[p], hh))
        pv.append(pv_h)
    pv = jnp.concatenate(pv, axis=1)
    l_new = alpha * l_s[:, 0:1] + jnp.sum(pf, axis=-1, keepdims=True)
    acc = alpha * acc_s[...] + pv
    m_s[...] = jnp.broadcast_to(m_new, m_s.shape)
    l_s[...] = jnp.broadcast_to(l_new, l_s.shape)
    acc_s[...] = acc

    @pl.when(j == pl.num_programs(1) - 1)
    def _():
        s_new = jnp.sum(qb_rows * kbn_ref[...], axis=-1, keepdims=True) * scale_b
        m_f = jnp.maximum(m_new, s_new)
        a_f = jnp.exp(m_new - m_f)
        p_new = jnp.exp(s_new - m_f)
        o = (a_f * acc + p_new * vbn_ref[...]) / (a_f * l_new + p_new)
        coef = jnp.where(row % 2 == 0, 1.0, -lam_ref[0])
        ob = jnp.sum(jnp.where(lane // HEAD_W == row // 2, o * coef, 0.0), axis=0, keepdims=True)
        for hh in range(N_HEADS):
            sl = slice(hh * HEAD_W, (hh + 1) * HEAD_W)
            ob_o[:, sl] = _subln(ob[:, sl], gsub_ref[...]).astype(BF16)
        g = jnp.where(lane128 < n_past_blocks, gate * (1.0 / MOBA_BLOCK), NEG_INF)
        picks = jnp.zeros((rows, HEAD_W), I32)
        for t in range(MOBA_TOPK):
            mx = jnp.max(g, axis=-1, keepdims=True)
            idx = jnp.min(jnp.where(g == mx, lane128, HEAD_W), axis=-1, keepdims=True)
            picks = jnp.where(lane128 == t, idx, picks)
            g = jnp.where(lane128 == idx, NEG_INF, g)
        sel_o[...] = picks


def _sample_gather_kernel(page_ref, qa_ref, kan_ref, van_ref, sa_ref, va_hbm, oa_o, vbuf, sems, *, n, page_size):
    i = pl.program_id(0)
    pages_per_block = MOBA_BLOCK // page_size
    n_sel = N_HEADS * MOBA_TOPK * pages_per_block
    slot = i % 2

    def v_copy(seq, c, s):
        return pltpu.make_async_copy(va_hbm.at[page_ref[seq * 2 * n_sel + c]], vbuf.at[s, c], sems.at[s])

    def start_pages(seq, s):
        for c in range(n_sel):
            v_copy(seq, c, s).start()

    @pl.when(i == 0)
    def _():
        start_pages(0, 0)

    if n > 1:
        @pl.when(i + 1 < n)
        def _():
            start_pages(i + 1, 1 - slot)

    for c in range(n_sel):
        v_copy(i, c, slot).wait()

    scale_a = HEAD_W ** -0.5
    qa = qa_ref[...].astype(F32)
    for hh in range(N_HEADS):
        sl = slice(hh * HEAD_W, (hh + 1) * HEAD_W)
        c0 = hh * MOBA_TOPK * pages_per_block
        s = jnp.concatenate([sa_ref[page_ref[i * 2 * n_sel + n_sel + c0 + c], hh:hh + 1, :]
                             for c in range(MOBA_TOPK * pages_per_block)], axis=1) * scale_a
        s_self = jnp.sum(qa[:, sl] * kan_ref[:, sl], axis=-1, keepdims=True) * scale_a
        m = jnp.maximum(jnp.max(s, axis=-1, keepdims=True), s_self)
        p = jnp.exp(s - m)
        p_self = jnp.exp(s_self - m)
        l = jnp.sum(p, axis=-1, keepdims=True) + p_self
        p16 = jnp.broadcast_to(p, (8, p.shape[1])).astype(BF16)
        acc = p_self * van_ref[:, sl]
        for c in range(MOBA_TOPK * pages_per_block):
            v = vbuf[slot, c0 + c, pl.ds(hh, page_size, stride=N_HEADS), :].astype(BF16)
            acc = acc + _dot(p16[:, c * page_size:(c + 1) * page_size], v)[0:1, :]
        oa_o[:, sl] = (acc / l).astype(BF16)


def _attention_sample(lam, page_table, qa, qb, ka_new, va_new, kb_new, vb_new, gsub,
                      cache_ka, cache_va, cache_kb, cache_vb):
    ns, n_pages = page_table.shape
    n_pool, page_rows = cache_ka.shape[0], cache_ka.shape[1]
    page_size = page_rows // N_HEADS
    past = n_pages * page_size
    assert page_size == HEAD_W, "transposed diff-k pages share the (SEC, HEAD_W) page block shape"
    assert past % MOBA_BLOCK == 0 and MOBA_BLOCK % page_size == 0 and n_pages % PAGES_PER_STEP == 0
    n_past_blocks = past // MOBA_BLOCK
    assert MOBA_TOPK <= n_past_blocks <= HEAD_W
    pages_per_block = MOBA_BLOCK // page_size
    npg = PAGES_PER_STEP
    rows = 2 * N_HEADS
    pt_flat = page_table.reshape(-1)
    r3 = lambda a: a.reshape(ns, 1, SEC)
    tok = pl.BlockSpec((None, 1, SEC), lambda s, j, pt: (s, 0, 0))
    smem = pl.BlockSpec(memory_space=pltpu.SMEM)

    def page_spec(p):
        return pl.BlockSpec((None, page_rows, HEAD_W), lambda s, j, pt: (pt[s * n_pages + j * npg + p], 0, 0))

    pages = [page_spec(p) for p in range(npg)]
    scan = pl.pallas_call(
        functools.partial(_sample_scan_kernel, page_size=page_size, n_past_blocks=n_past_blocks),
        grid_spec=pltpu.PrefetchScalarGridSpec(
            num_scalar_prefetch=1, grid=(ns, n_pages // npg),
            in_specs=[smem, tok, tok, tok, tok, pl.BlockSpec((1, HEAD_W), lambda s, j, pt: (0, 0))] + pages * 3,
            out_specs=[pl.BlockSpec((None, n_pages, rows, page_size), lambda s, j, pt: (s, 0, 0, 0)),
                       pl.BlockSpec((None, rows, HEAD_W), lambda s, j, pt: (s, 0, 0)),
                       tok],
            scratch_shapes=[pltpu.VMEM((rows, HEAD_W), F32)] * 2 + [pltpu.VMEM((rows, SEC), F32),
                                                                    pltpu.VMEM((rows, HEAD_W), F32)]),
        out_shape=[jax.ShapeDtypeStruct((ns, n_pages, rows, page_size), F32),
                   jax.ShapeDtypeStruct((ns, rows, HEAD_W), I32),
                   jax.ShapeDtypeStruct((ns, 1, SEC), BF16)],
        compiler_params=pltpu.CompilerParams(dimension_semantics=("parallel", "arbitrary")),
        name="sample_scan",
    )
    sa_all, picks, ob = scan(pt_flat, lam, r3(qa), r3(qb), r3(kb_new), r3(vb_new), gsub,
                             *([cache_ka] * npg), *([cache_kb] * npg), *([cache_vb] * npg))
    sel_flat = picks[:, :N_HEADS, :MOBA_TOPK].reshape(-1)
    oa = _sample_gather(pt_flat, sel_flat, sa_all, qa, ka_new, va_new, cache_va, n_pages)
    return oa, ob.reshape(ns, SEC)


def _sample_gather(pt_flat, sel_flat, sa_all, qa, ka_new, va_new, cache_va, n_pages):
    ns = qa.shape[0]
    page_rows = cache_va.shape[1]
    page_size = page_rows // N_HEADS
    pages_per_block = MOBA_BLOCK // page_size
    rows = 2 * N_HEADS
    n_sel = N_HEADS * MOBA_TOPK * pages_per_block
    r3 = lambda a: a.reshape(ns, 1, SEC)
    tok1 = pl.BlockSpec((None, 1, SEC), lambda s, pg: (s, 0, 0))
    blocks = jnp.clip(sel_flat.reshape(ns, N_HEADS * MOBA_TOPK), 0, n_pages // pages_per_block - 1)
    pos = (blocks[:, :, None] * pages_per_block + jnp.arange(pages_per_block, dtype=I32)).reshape(ns, n_sel)
    pool = jnp.take_along_axis(pt_flat.reshape(ns, n_pages), pos, axis=1)
    page_tab = jnp.concatenate([pool, pos], axis=1).reshape(-1).astype(I32)
    gather = pl.pallas_call(
        functools.partial(_sample_gather_kernel, n=ns, page_size=page_size),
        grid_spec=pltpu.PrefetchScalarGridSpec(
            num_scalar_prefetch=1, grid=(ns,),
            in_specs=[tok1, tok1, tok1,
                      pl.BlockSpec((None, n_pages, rows, page_size), lambda s, pg: (s, 0, 0, 0)),
                      pl.BlockSpec(memory_space=pl.ANY)],
            out_specs=tok1,
            scratch_shapes=[pltpu.VMEM((2, n_sel, page_rows, HEAD_W), F32), pltpu.SemaphoreType.DMA((2,))]),
        out_shape=jax.ShapeDtypeStruct((ns, 1, SEC), BF16),
        compiler_params=pltpu.CompilerParams(dimension_semantics=("arbitrary",), vmem_limit_bytes=VMEM_LIMIT),
        name="sample_gather",
    )
    oa = gather(page_tab, r3(qa), r3(ka_new), r3(va_new), sa_all, cache_va)
    return oa.reshape(ns, SEC)


def _mem_sample_kernel(qm_ref, mk_ref, mv_ref, om_o):
    n_mem = mk_ref.shape[0] // N_HEADS
    qm = qm_ref[...].astype(F32)
    for hh in range(N_HEADS):
        sl = slice(hh * HEAD_W, (hh + 1) * HEAD_W)
        q16 = jnp.broadcast_to(qm[:, sl], (8, HEAD_W)).astype(BF16)
        k16 = mk_ref[pl.ds(hh, n_mem, stride=N_HEADS), :].astype(BF16)
        v16 = mv_ref[pl.ds(hh, n_mem, stride=N_HEADS), :].astype(BF16)
        _, l, acc = _softmax_start(_nt(q16, k16) * (HEAD_W ** -0.5), v16)
        om_o[:, sl] = (acc / l)[0:1, :].astype(BF16)


def _attention_mem_sample(qm, mk, mv):
    ns, mem_rows, _ = mk.shape
    tok = pl.BlockSpec((None, 1, SEC), lambda s: (s, 0, 0))
    kv = pl.BlockSpec((None, mem_rows, HEAD_W), lambda s: (s, 0, 0))
    return pl.pallas_call(
        _mem_sample_kernel, grid=(ns,), in_specs=[tok, kv, kv], out_specs=tok,
        out_shape=jax.ShapeDtypeStruct((ns, 1, SEC), BF16),
        compiler_params=pltpu.CompilerParams(dimension_semantics=("parallel",)),
        name="mem_sample",
    )(qm.reshape(ns, 1, SEC), mk, mv).reshape(ns, SEC)


def _merge_kernel(*refs, n_main, n_steps):
    i = pl.program_id(0)
    hn_o = refs[-3]

    @pl.when(i < n_main)
    def _():
        _merge_tile(*refs)

    if n_main < n_steps:
        @pl.when(i >= n_main)
        def _():
            hn_o[...] = jnp.zeros(hn_o.shape, hn_o.dtype)


def _merge_tile(x_ref, oa_ref, ob_ref, om_ref, g_ref, wa_ref, wb_ref, wm_ref, wo_ref, gffn_ref,
                wrh_ref, wrl_ref, br_ref, *rest):
    xmid_o, hn_o, idx_o, gate_o = rest[-4:]
    d = x_ref.shape[1]
    n_exp = wrh_ref.shape[0]
    merged = (g_ref[:, 0:d].astype(F32) * _dot(oa_ref[...], wa_ref[...])
              + g_ref[:, d:2 * d].astype(F32) * _dot(ob_ref[...], wb_ref[...])
              + g_ref[:, 2 * d:3 * d].astype(F32) * _dot(om_ref[...], wm_ref[...]))
    xm = x_ref[...] + _dot(merged.astype(BF16), wo_ref[...])
    xmid_o[...] = xm
    hn = _rms(xm, RMS_EPS) * gffn_ref[...]
    hn_o[...] = hn
    hi = hn.astype(BF16)
    lo = (hn - hi.astype(F32)).astype(BF16)
    g = _nt(wrh_ref[...], hi) + _nt(wrl_ref[...], hi) + _nt(wrh_ref[...], lo) + br_ref[...]
    erow = lax.broadcasted_iota(I32, g.shape, 0)
    vals, idxs = [], []
    for _ in range(MOE_TOPK):
        mx = jnp.max(g, axis=0, keepdims=True)
        ix = jnp.min(jnp.where(g == mx, erow, n_exp), axis=0, keepdims=True)
        vals.append(mx)
        idxs.append(ix)
        g = jnp.where(erow == ix, NEG_INF, g)
    e = jnp.exp(jnp.concatenate(vals, axis=0) - vals[0])
    gate_o[...] = e / jnp.sum(e, axis=0, keepdims=True)
    idx_o[...] = jnp.concatenate(idxs, axis=0)


def _merge(x2d, oa, ob, om, g, tm, w_a, w_b, w_m, w_o, gffn, wr_hi, wr_lo, br, hn_all, row_block0):
    n, d = x2d.shape
    n_exp = wr_hi.shape[0]
    n_main = n // tm
    creates = isinstance(hn_all, tuple)
    n_all = hn_all[0] if creates else hn_all.shape[0]
    n_steps = pl.cdiv(n_all, tm) if creates else n_main
    last = n_main - 1
    row = lambda w: pl.BlockSpec((tm, w), lambda i: (jnp.minimum(i, last), 0))
    col = pl.BlockSpec((MOE_TOPK, tm), lambda i: (0, jnp.minimum(i, last)))
    in_specs = [row(d), row(SEC), row(SEC), row(SEC), row(3 * d),
                _const_spec((SEC, d)), _const_spec((SEC, d)), _const_spec((SEC, d)), _const_spec((d, d)),
                _const_spec((1, d)), _const_spec((n_exp, d)), _const_spec((n_exp, d)), _const_spec((n_exp, 1))]
    args = [x2d, oa, ob, om, g, w_a, w_b, w_m, w_o, gffn, wr_hi, wr_lo, br]
    aliases = {}
    if not creates:
        in_specs.append(pl.BlockSpec(memory_space=pl.ANY))
        args.append(hn_all)
        aliases = {len(args) - 1: 1}
    return pl.pallas_call(
        functools.partial(_merge_kernel, n_main=n_main, n_steps=n_steps), grid=(n_steps,), in_specs=in_specs,
        out_specs=[row(d), pl.BlockSpec((tm, d), lambda i: (row_block0 + i, 0)), col, col],
        out_shape=[jax.ShapeDtypeStruct((n, d), F32), jax.ShapeDtypeStruct((n_all, d), F32),
                   jax.ShapeDtypeStruct((MOE_TOPK, n), I32), jax.ShapeDtypeStruct((MOE_TOPK, n), F32)],
        input_output_aliases=aliases,
        compiler_params=pltpu.CompilerParams(dimension_semantics=("arbitrary",), vmem_limit_bytes=VMEM_LIMIT),
        name="merge",
    )(*args)


ROW_COPY_UNROLL = 8


def _start_row_copies(src_hbm, idx_ref, nrow, dst, sem):
    def body(g, c):
        for u in range(ROW_COPY_UNROLL):
            r = g * ROW_COPY_UNROLL + u
            pltpu.make_async_copy(src_hbm.at[pl.ds(idx_ref[0, r], 1), :], dst.at[pl.ds(r, 1), :], sem).start()
        return c
    lax.fori_loop(0, nrow // ROW_COPY_UNROLL, body, 0)


def _expert_kernel(be_ref, src_ref, nxt_ref, gate_ref, hn_hbm, w1_ref, b1_ref, w2_ref, b2_ref, y_o, xbuf, sems,
                   *, n):
    i = pl.program_id(0)
    nrow = xbuf.shape[1]
    d_ff = w2_ref.shape[0]
    slot = i % 2

    def start_rows(idx_ref, s):
        _start_row_copies(hn_hbm, idx_ref, nrow, xbuf.at[s], sems.at[s])

    @pl.when(i == 0)
    def _():
        start_rows(src_ref, 0)

    if n > 1:
        @pl.when(i + 1 < n)
        def _():
            start_rows(nxt_ref, 1 - slot)

    pltpu.make_async_copy(hn_hbm.at[pl.ds(0, nrow), :], xbuf.at[slot], sems.at[slot]).wait()
    x = xbuf[slot].astype(BF16)
    u = _dot(x, w1_ref[...]) + b1_ref[...]
    u_glu = jnp.minimum(u[:, :d_ff], SWIGLU_LIMIT)
    u_lin = jnp.clip(u[:, d_ff:], -SWIGLU_LIMIT, SWIGLU_LIMIT)
    act = u_glu * (1.0 / (1.0 + jnp.exp(-SWIGLU_ALPHA * u_glu))) * (u_lin + 1.0)
    y = _dot(act.astype(BF16), w2_ref[...]) + b2_ref[...]
    eye = lax.broadcasted_iota(I32, (nrow, nrow), 0) == lax.broadcasted_iota(I32, (nrow, nrow), 1)
    gcol = jnp.sum(jnp.where(eye, gate_ref[...], 0.0), axis=-1, keepdims=True)
    y_o[...] = y * gcol


def _experts(blk_expert, row_src, row_gate, hn_all, w1, b1, w2, b2):
    n_blocks = blk_expert.shape[0]
    n_exp, d, f2 = w1.shape
    d_ff = f2 // 2
    src3 = row_src.reshape(n_blocks, 1, MOE_BLOCK)
    gate3 = row_gate.reshape(n_blocks, 1, MOE_BLOCK)
    idx_spec = lambda off: pl.BlockSpec((None, 1, MOE_BLOCK),
                                        lambda i, be: (jnp.minimum(i + off, n_blocks - 1), 0, 0),
                                        memory_space=pltpu.SMEM)
    return pl.pallas_call(
        functools.partial(_expert_kernel, n=n_blocks),
        grid_spec=pltpu.PrefetchScalarGridSpec(
            num_scalar_prefetch=1, grid=(n_blocks,),
            in_specs=[idx_spec(0), idx_spec(1),
                      pl.BlockSpec((None, 1, MOE_BLOCK), lambda i, be: (i, 0, 0)),
                      pl.BlockSpec(memory_space=pl.ANY),
                      pl.BlockSpec((None, d, f2), lambda i, be: (be[i], 0, 0)),
                      pl.BlockSpec((None, 1, f2), lambda i, be: (be[i], 0, 0)),
                      pl.BlockSpec((None, d_ff, d), lambda i, be: (be[i], 0, 0)),
                      pl.BlockSpec((None, 1, d), lambda i, be: (be[i], 0, 0))],
            out_specs=pl.BlockSpec((MOE_BLOCK, d), lambda i, be: (i, 0)),
            scratch_shapes=[pltpu.VMEM((2, MOE_BLOCK, d), F32), pltpu.SemaphoreType.DMA((2,))]),
        out_shape=jax.ShapeDtypeStruct((n_blocks * MOE_BLOCK, d), F32),
        compiler_params=pltpu.CompilerParams(dimension_semantics=("arbitrary",), vmem_limit_bytes=VMEM_LIMIT),
        name="experts",
    )(blk_expert, src3, src3, gate3, hn_all, w1, b1.reshape(n_exp, 1, f2), w2, b2.reshape(n_exp, 1, d))


def _combine_kernel(pos_ref, nxt_ref, xmid_ref, ys_hbm, y_o, buf, sems, *, n):
    i = pl.program_id(0)
    tm = xmid_ref.shape[0]
    nrow = MOE_TOPK * tm
    slot = i % 2

    def start_rows(idx_ref, s):
        _start_row_copies(ys_hbm, idx_ref, nrow, buf.at[s], sems.at[s])

    @pl.when(i == 0)
    def _():
        start_rows(pos_ref, 0)

    if n > 1:
        @pl.when(i + 1 < n)
        def _():
            start_rows(nxt_ref, 1 - slot)

    pltpu.make_async_copy(ys_hbm.at[pl.ds(0, nrow), :], buf.at[slot], sems.at[slot]).wait()
    acc = xmid_ref[...]
    for k in range(MOE_TOPK):
        acc = acc + buf[slot, k * tm:(k + 1) * tm, :]
    y_o[...] = acc


def _combine(pos_kt, xmid, ys, tm):
    n, d = xmid.shape
    n_tiles = n // tm
    pos3 = pos_kt.reshape(MOE_TOPK, n_tiles, tm).transpose(1, 0, 2).reshape(n_tiles, 1, MOE_TOPK * tm)
    idx_spec = lambda off: pl.BlockSpec((None, 1, MOE_TOPK * tm),
                                        lambda i: (jnp.minimum(i + off, n_tiles - 1), 0, 0),
                                        memory_space=pltpu.SMEM)
    return pl.pallas_call(
        functools.partial(_combine_kernel, n=n_tiles), grid=(n_tiles,),
        in_specs=[idx_spec(0), idx_spec(1), pl.BlockSpec((tm, d), lambda i: (i, 0)),
                  pl.BlockSpec(memory_space=pl.ANY)],
        out_specs=pl.BlockSpec((tm, d), lambda i: (i, 0)),
        out_shape=jax.ShapeDtypeStruct((n, d), F32),
        scratch_shapes=[pltpu.VMEM((2, MOE_TOPK * tm, d), F32), pltpu.SemaphoreType.DMA((2,))],
        compiler_params=pltpu.CompilerParams(dimension_semantics=("arbitrary",)),
        name="combine",
    )(pos3, pos3, xmid, ys)


def _route(e_tk, g_tk, n_exp):
    n_tok = e_tk.shape[0]
    n_as = n_tok * MOE_TOPK
    e_flat = e_tk.reshape(n_as)
    g_flat = g_tk.reshape(n_as)
    order = jnp.argsort(e_flat, stable=True)
    e_sorted = e_flat[order]
    counts = jnp.bincount(e_flat, length=n_exp)
    padded = (counts + MOE_BLOCK - 1) // MOE_BLOCK * MOE_BLOCK
    pad_end = jnp.cumsum(padded)
    pad_start = pad_end - padded
    start = jnp.cumsum(counts) - counts
    dest = (pad_start[e_sorted] + jnp.arange(n_as, dtype=I32) - start[e_sorted]).astype(I32)
    n_blocks = -(-n_as // MOE_BLOCK) + n_exp
    blk_start = jnp.arange(n_blocks, dtype=I32) * MOE_BLOCK
    blk_expert = jnp.minimum(jnp.sum(pad_end[None, :] <= blk_start[:, None], axis=1), n_exp - 1).astype(I32)
    within = (blk_start - pad_start[blk_expert].astype(I32))[:, None] + jnp.arange(MOE_BLOCK, dtype=I32)[None, :]
    valid = within < counts[blk_expert].astype(I32)[:, None]
    src = jnp.clip(within + start[blk_expert].astype(I32)[:, None], 0, n_as - 1).reshape(-1)
    assign = order[src].astype(I32)
    valid = valid.reshape(-1)
    row_src = jnp.where(valid, assign // MOE_TOPK, 0)
    row_gate = jnp.where(valid, g_flat[assign], 0.0)
    _, pos = lax.sort_key_val(order.astype(I32), dest)
    return blk_expert, row_src, row_gate, pos.reshape(n_tok, MOE_TOPK).T


def kernel(x_prompt, x_sample, mem_prompt, cache_moba_k, cache_moba_v, cache_diff_k, cache_diff_v, cache_mem_k, cache_mem_v, page_table, w_in, b_gate, g_norm_mix, g_qa, g_ka, g_qb, g_kb, lambda_q1, lambda_k1, lambda_q2, lambda_k2, g_subln, g_norm_mem, w_mem_kv, g_qm, g_km, w_br_a, w_br_b, w_br_m, w_out, g_norm_ffn, w_router, b_router, w1, b1, w2, b2):
    assert w_in.shape[0] == 1, "single-layer trunk"
    batch, seq, d = x_prompt.shape
    ns, dec_seq, _ = x_sample.shape
    assert dec_seq == 1 and seq % MOBA_BLOCK == 0 and seq // MOBA_BLOCK >= MOBA_TOPK
    n_mem = mem_prompt.shape[1]
    n_pages, page_size = page_table.shape[1], cache_moba_k.shape[2]
    n_pool = cache_moba_k.shape[1]
    np_tok = batch * seq
    n_exp = w_router.shape[-1]
    row1 = lambda a: a.reshape(1, -1).astype(F32)

    lam = (jnp.exp(jnp.sum(lambda_q1[0].astype(F32) * lambda_k1[0].astype(F32)))
           - jnp.exp(jnp.sum(lambda_q2[0].astype(F32) * lambda_k2[0].astype(F32))) + LAM_INIT).reshape(1)
    w_in16 = w_in[0].astype(BF16)
    gqb2, gkb2 = (row1(jnp.tile(g[0], HEAD_W // DH_B)) for g in (g_qb, g_kb))
    proj_w = (w_in16, row1(g_norm_mix), row1(b_gate), row1(g_qa), row1(g_ka), gqb2, gkb2, row1(g_qm))
    gsub = row1(g_subln)

    tm_p = 512 if seq % 512 == 0 else MOBA_BLOCK
    pos_p = jnp.arange(seq, dtype=I32)
    qa, ka, va, qb, kb, vb, qm, gates = _project(
        x_prompt.reshape(np_tok, d), tm_p, seq // tm_p, *proj_w, _rope_tables(pos_p, HEAD_W), _rope_tables(pos_p, DH_B))
    mk, mv = _memory_kv(mem_prompt.reshape(batch * n_mem, d), min(512, batch * n_mem), w_mem_kv[0].astype(BF16),
                        row1(g_norm_mem), row1(g_km))
    oa, ob, om = _attention_prompt(lam, qa, qb, qm, ka, va, kb, vb, mk, mv, gsub, batch, seq, n_mem)

    pos_s = jnp.full((ns,), n_pages * page_size, dtype=I32)
    qa_s, ka_s, va_s, qb_s, kb_s, vb_s, qm_s, gates_s = _project(
        x_sample.reshape(ns, d), ns, 1, *proj_w, _rope_tables(pos_s, HEAD_W), _rope_tables(pos_s, DH_B))
    key_head_rows = lambda c: c.reshape(c.shape[1], c.shape[2] * N_HEADS, HEAD_W)
    diff_kt = jnp.transpose(cache_diff_k[0], (0, 2, 3, 4, 1)).reshape(n_pool, SEC, page_size)
    oa_s, ob_s = _attention_sample(lam, page_table, qa_s, qb_s, ka_s, va_s, kb_s, vb_s, gsub,
                                   key_head_rows(cache_moba_k), key_head_rows(cache_moba_v),
                                   diff_kt, key_head_rows(cache_diff_v))
    om_s = _attention_mem_sample(qm_s, key_head_rows(cache_mem_k), key_head_rows(cache_mem_v))

    wr = w_router[0].T.astype(F32)
    wr_hi = wr.astype(BF16)
    wr_lo = (wr - wr_hi.astype(F32)).astype(BF16)
    merge_w = (w_br_a[0].astype(BF16), w_br_b[0].astype(BF16), w_br_m[0].astype(BF16), w_out[0].astype(BF16),
               row1(g_norm_ffn), wr_hi, wr_lo, b_router[0].reshape(n_exp, 1).astype(F32))
    n_all = np_tok + ns
    xmid_p, hn_all, e_p, g_p = _merge(x_prompt.reshape(np_tok, d), oa, ob, om, gates, tm_p, *merge_w, (n_all,), 0)
    xmid_s, hn_all, e_s, g_s = _merge(x_sample.reshape(ns, d), oa_s, ob_s, om_s, gates_s, ns, *merge_w, hn_all,
                                      np_tok // ns)

    e_tk = jnp.concatenate([e_p, e_s], axis=1).T
    g_tk = jnp.concatenate([g_p, g_s], axis=1).T
    blk_expert, row_src, row_gate, pos_kt = _route(e_tk, g_tk, n_exp)
    ys = _experts(blk_expert, row_src, row_gate, hn_all, w1[0].astype(BF16), b1[0].astype(F32),
                  w2[0].astype(BF16), b2[0].astype(F32))
    y_p = _combine(pos_kt[:, :np_tok], xmid_p, ys, 128)
    y_s = _combine(pos_kt[:, np_tok:], xmid_s, ys, ns)

    k5 = lambda a, n: a.reshape(1, n, -1, N_HEADS, HEAD_W)
    k6 = lambda a, n: a.reshape(1, n, -1, N_HEADS, 2, DH_B)
    return (y_p.reshape(batch, seq, d), y_s.reshape(ns, 1, d),
            k5(ka, batch), k5(va, batch), k6(kb, batch), k5(vb, batch),
            k5(mk, batch), k5(mv, batch),
            k5(ka_s, ns), k5(va_s, ns), k6(kb_s, ns), k5(vb_s, ns))
```

```python
import functools
import math

import jax
import jax.numpy as jnp
from jax import lax
from jax.experimental import pallas as pl
from jax.experimental.pallas import tpu as pltpu

F32 = jnp.float32
BF16 = jnp.bfloat16
I32 = jnp.int32

N_HEADS = 4
HEAD_W = 128
SEC = N_HEADS * HEAD_W
DH_B = 64
MOBA_BLOCK = 256
MOBA_TOPK = 3
ROPE_THETA = 500000.0
ROT_FRACTION = 4
RMS_EPS = 1e-6
SUBLN_EPS = 1e-5
LAM_INIT = 0.8 - 0.6 * math.exp(-0.3 * 0)
MOE_TOPK = 4
MOE_BLOCK = 256
SWIGLU_ALPHA = 1.702
SWIGLU_LIMIT = 7.0
PAGES_PER_STEP = 8
NEG_INF = float("-inf")
VMEM_LIMIT = 56 * 1024 * 1024

_NT = (((1,), (1,)), ((), ()))


def _nt(a, b):
    return lax.dot_general(a, b, _NT, preferred_element_type=F32)


def _dot(a, b):
    return jnp.dot(a, b, preferred_element_type=F32)


def _rms(x, eps):
    return x * lax.rsqrt(jnp.mean(x * x, axis=-1, keepdims=True) + eps)


def _headnorm128(z, g):
    return _rms(z, RMS_EPS) * g


def _headnorm64(z, g):
    zz = z * z
    lo = lax.broadcasted_iota(I32, z.shape, 1) < DH_B
    tot = jnp.sum(zz, axis=-1, keepdims=True)
    s_lo = jnp.sum(jnp.where(lo, zz, 0.0), axis=-1, keepdims=True)
    s_hi = tot - s_lo
    r = jnp.where(lo, lax.rsqrt(s_lo * (1.0 / DH_B) + RMS_EPS), lax.rsqrt(s_hi * (1.0 / DH_B) + RMS_EPS))
    return z * r * g


def _rope(y, c, s_up, s_dn, half):
    return y * c + pltpu.roll(y, HEAD_W - half, 1) * s_up + pltpu.roll(y, half, 1) * s_dn


def _rope_tables(pos, d):
    rot = d // ROT_FRACTION
    half = rot // 2
    inv = ROPE_THETA ** (-(jnp.arange(half, dtype=F32) * 2.0 / rot))
    ang = pos.astype(F32)[:, None] * inv[None, :]
    cos, sin = jnp.cos(ang), jnp.sin(ang)
    p = pos.shape[0]
    c = jnp.concatenate([cos, cos, jnp.ones((p, d - rot), F32)], axis=-1)
    s_up = jnp.concatenate([-sin, jnp.zeros((p, d - half), F32)], axis=-1)
    s_dn = jnp.concatenate([jnp.zeros((p, half), F32), sin, jnp.zeros((p, d - rot), F32)], axis=-1)
    rep = HEAD_W // d
    return tuple(jnp.tile(t, (1, rep)) for t in (c, s_up, s_dn))


def _proj_kernel(x_ref, w_ref, gmix_ref, bg_ref, gqa_ref, gka_ref, gqb_ref, gkb_ref, gqm_ref,
                 ca_ref, sua_ref, sda_ref, cb_ref, sub_ref, sdb_ref,
                 qa_o, ka_o, va_o, qb_o, kb_o, vb_o, qm_o, g_o):
    h = (_rms(x_ref[...], RMS_EPS) * gmix_ref[...]).astype(BF16)

    def sec(k):
        return _dot(h, w_ref[:, k * SEC:(k + 1) * SEC])

    def per_head(z, fn, out):
        for hh in range(N_HEADS):
            sl = slice(hh * HEAD_W, (hh + 1) * HEAD_W)
            out[:, sl] = fn(z[:, sl]).astype(out.dtype)

    def rope_a(y):
        return _rope(y, ca_ref[...], sua_ref[...], sda_ref[...], HEAD_W // ROT_FRACTION // 2)

    def rope_b(y):
        return _rope(y, cb_ref[...], sub_ref[...], sdb_ref[...], DH_B // ROT_FRACTION // 2)

    per_head(sec(0), lambda z: rope_a(_headnorm128(z, gqa_ref[...])), qa_o)
    per_head(sec(1), lambda z: rope_a(_headnorm128(z, gka_ref[...])), ka_o)
    va_o[...] = sec(2)
    per_head(sec(3), lambda z: rope_b(_headnorm64(z, gqb_ref[...])), qb_o)
    per_head(sec(4), lambda z: rope_b(_headnorm64(z, gkb_ref[...])), kb_o)
    vb_o[...] = sec(5)
    per_head(sec(6), lambda z: _headnorm128(z, gqm_ref[...]), qm_o)
    n_gate_sec = g_o.shape[1] // SEC
    for k in range(n_gate_sec):
        z = sec(7 + k) + bg_ref[:, k * SEC:(k + 1) * SEC]
        g_o[:, k * SEC:(k + 1) * SEC] = (1.0 / (1.0 + jnp.exp(-z))).astype(BF16)


def _const_spec(shape):
    return pl.BlockSpec(shape, lambda *_: (0,) * len(shape), pipeline_mode=pl.Buffered(1))


def _project(x2d, tm, n_pos_blocks, w16, gmix, bgate, gqa, gka, gqb2, gkb2, gqm, tabs_a, tabs_b):
    n, d = x2d.shape
    ncol = w16.shape[1]
    n_gate = ncol - 7 * SEC
    row = lambda w: pl.BlockSpec((tm, w), lambda i: (i, 0))
    tab = pl.BlockSpec((tm, HEAD_W), lambda i: (i % n_pos_blocks, 0))
    in_specs = [row(d), _const_spec((d, ncol)), _const_spec((1, d)), _const_spec((1, n_gate))]
    in_specs += [_const_spec((1, HEAD_W))] * 5 + [tab] * 6
    out_shape = [jax.ShapeDtypeStruct((n, SEC), dt) for dt in (BF16, F32, F32, BF16, F32, F32, BF16)]
    out_shape.append(jax.ShapeDtypeStruct((n, n_gate), BF16))
    out_specs = [row(SEC)] * 7 + [row(n_gate)]
    return pl.pallas_call(
        _proj_kernel, grid=(n // tm,), in_specs=in_specs, out_specs=out_specs, out_shape=out_shape,
        compiler_params=pltpu.CompilerParams(dimension_semantics=("parallel",), vmem_limit_bytes=VMEM_LIMIT),
        name="proj",
    )(x2d, w16, gmix, bgate, gqa, gka, gqb2, gkb2, gqm, *tabs_a, *tabs_b)


def _memkv_kernel(x_ref, w_ref, gn_ref, gkm_ref, k_o, v_o):
    h = (_rms(x_ref[...], RMS_EPS) * gn_ref[...]).astype(BF16)
    zk = _dot(h, w_ref[:, :SEC])
    for hh in range(N_HEADS):
        sl = slice(hh * HEAD_W, (hh + 1) * HEAD_W)
        k_o[:, sl] = _headnorm128(zk[:, sl], gkm_ref[...])
    v_o[...] = _dot(h, w_ref[:, SEC:])


def _memory_kv(mem2d, tm, w16, gn, gkm):
    n, d = mem2d.shape
    row = lambda w: pl.BlockSpec((tm, w), lambda i: (i, 0))
    return pl.pallas_call(
        _memkv_kernel, grid=(n // tm,),
        in_specs=[row(d), _const_spec((d, 2 * SEC)), _const_spec((1, d)), _const_spec((1, HEAD_W))],
        out_specs=[row(SEC), row(SEC)],
        out_shape=[jax.ShapeDtypeStruct((n, SEC), F32)] * 2,
        compiler_params=pltpu.CompilerParams(dimension_semantics=("parallel",)),
        name="memkv",
    )(mem2d, w16, gn, gkm)


def _softmax_start(s, v16):
    m = jnp.max(s, axis=-1, keepdims=True)
    p = jnp.exp(s - m)
    return m, jnp.sum(p, axis=-1, keepdims=True), _dot(p.astype(BF16), v16)


def _softmax_update(s, v16, m, l, acc):
    m_new = jnp.maximum(m, jnp.max(s, axis=-1, keepdims=True))
    alpha = jnp.exp(m - m_new)
    p = jnp.exp(s - m_new)
    return m_new, alpha * l + jnp.sum(p, axis=-1, keepdims=True), alpha * acc + _dot(p.astype(BF16), v16)


def _subln(o, g):
    return _rms(o, SUBLN_EPS) * g * (1.0 - LAM_INIT)


def _attn_prompt_kernel(lam_ref, qa_ref, qb_ref, qm_ref, ka_ref, va_ref, kb_ref, vb_ref, mk_ref, mv_ref,
                        gsub_ref, oa_o, ob_o, om_o, ka16, va16, kb16, vb16, mk16, mv16, kmean16):
    qi = pl.program_id(2)
    seq = ka_ref.shape[0]
    tq = qa_ref.shape[0]
    nb = seq // MOBA_BLOCK
    lam = lam_ref[0]

    @pl.when(qi == 0)
    def _():
        ka = ka_ref[...]
        ka16[...] = ka.astype(BF16)
        va16[...] = va_ref[...].astype(BF16)
        kb16[...] = kb_ref[...].astype(BF16)
        vb16[...] = vb_ref[...].astype(BF16)
        mk16[...] = mk_ref[...].astype(BF16)
        mv16[...] = mv_ref[...].astype(BF16)
        km = jnp.mean(ka.reshape(nb, MOBA_BLOCK, HEAD_W), axis=1)
        kmean16[...] = jnp.concatenate([km, jnp.zeros((HEAD_W - nb, HEAD_W), F32)], axis=0).astype(BF16)

    qa = qa_ref[...]
    qb = qb_ref[...]
    lane = lax.broadcasted_iota(I32, (tq, HEAD_W), 1)
    scale_a = HEAD_W ** -0.5
    scale_b = DH_B ** -0.5

    g = jnp.where(lane < qi, _nt(qa, kmean16[...]), NEG_INF)
    sel = jnp.zeros((tq, HEAD_W), F32)
    for _ in range(MOBA_TOPK):
        mx = jnp.max(g, axis=-1, keepdims=True)
        idx = jnp.min(jnp.where(g == mx, lane, HEAD_W), axis=-1, keepdims=True)
        pick = (lane == idx) & (mx > NEG_INF)
        sel = jnp.where(pick, 1.0, sel)
        g = jnp.where(pick, NEG_INF, g)

    qbf = qb.astype(F32)
    qb0 = jnp.where(lane < DH_B, qbf, 0.0).astype(BF16)
    qb1 = jnp.where(lane >= DH_B, qbf, 0.0).astype(BF16)

    causal = lax.broadcasted_iota(I32, (tq, tq), 1) <= lax.broadcasted_iota(I32, (tq, tq), 0)
    d0 = pl.multiple_of(qi * tq, tq)
    kb_d = kb16[pl.ds(d0, tq), :]
    vb_d = vb16[pl.ds(d0, tq), :]
    st_a = _softmax_start(jnp.where(causal, _nt(qa, ka16[pl.ds(d0, tq), :]) * scale_a, NEG_INF),
                          va16[pl.ds(d0, tq), :])
    st_0 = _softmax_start(jnp.where(causal, _nt(qb0, kb_d) * scale_b, NEG_INF), vb_d)
    st_1 = _softmax_start(jnp.where(causal, _nt(qb1, kb_d) * scale_b, NEG_INF), vb_d)

    def past_block(n, carry):
        st_a, st_0, st_1 = carry
        off = pl.multiple_of(n * tq, tq)
        chosen = jnp.sum(jnp.where(lane == n, sel, 0.0), axis=-1, keepdims=True) > 0.5
        sa = jnp.where(chosen, _nt(qa, ka16[pl.ds(off, tq), :]) * scale_a, NEG_INF)
        st_a = _softmax_update(sa, va16[pl.ds(off, tq), :], *st_a)
        kb_n = kb16[pl.ds(off, tq), :]
        vb_n = vb16[pl.ds(off, tq), :]
        st_0 = _softmax_update(_nt(qb0, kb_n) * scale_b, vb_n, *st_0)
        st_1 = _softmax_update(_nt(qb1, kb_n) * scale_b, vb_n, *st_1)
        return st_a, st_0, st_1

    st_a, st_0, st_1 = lax.fori_loop(0, qi, past_block, (st_a, st_0, st_1))

    oa_o[...] = (st_a[2] / st_a[1]).astype(BF16)
    ob = st_0[2] / st_0[1] - lam * (st_1[2] / st_1[1])
    ob_o[...] = _subln(ob, gsub_ref[...]).astype(BF16)

    sm = _nt(qm_ref[...], mk16[...]) * scale_a
    _, lm, accm = _softmax_start(sm, mv16[...])
    om_o[...] = (accm / lm).astype(BF16)


def _attention_prompt(lam, qa, qb, qm, ka, va, kb, vb, mk, mv, gsub, batch, seq, n_mem):
    r3 = lambda a, rows: a.reshape(batch, rows, SEC)
    tq = MOBA_BLOCK
    qspec = pl.BlockSpec((None, tq, HEAD_W), lambda b, h, q: (b, q, h))
    kspec = pl.BlockSpec((None, seq, HEAD_W), lambda b, h, q: (b, 0, h))
    mspec = pl.BlockSpec((None, n_mem, HEAD_W), lambda b, h, q: (b, 0, h))
    outs = pl.pallas_call(
        _attn_prompt_kernel, grid=(batch, N_HEADS, seq // tq),
        in_specs=[pl.BlockSpec(memory_space=pltpu.SMEM), qspec, qspec, qspec, kspec, kspec, kspec, kspec,
                  mspec, mspec, pl.BlockSpec((1, HEAD_W), lambda b, h, q: (0, 0))],
        out_specs=[qspec] * 3,
        out_shape=[jax.ShapeDtypeStruct((batch, seq, SEC), BF16)] * 3,
        scratch_shapes=[pltpu.VMEM((seq, HEAD_W), BF16)] * 4 + [pltpu.VMEM((n_mem, HEAD_W), BF16)] * 2
        + [pltpu.VMEM((HEAD_W, HEAD_W), BF16)],
        compiler_params=pltpu.CompilerParams(dimension_semantics=("parallel", "parallel", "arbitrary")),
        name="attn_prompt",
    )(lam, r3(qa, seq), r3(qb, seq), r3(qm, seq), r3(ka, seq), r3(va, seq), r3(kb, seq), r3(vb, seq),
      r3(mk, n_mem), r3(mv, n_mem), gsub)
    return [o.reshape(batch * seq, SEC) for o in outs]


def _sample_scan_kernel(pt_ref, lam_ref, qa_ref, qb_ref, kbn_ref, vbn_ref, gsub_ref, *rest,
                        page_size, n_past_blocks):
    npg = PAGES_PER_STEP
    ka_pages, kb_pages, vb_pages = rest[:npg], rest[npg:2 * npg], rest[2 * npg:3 * npg]
    sa_o, sel_o, ob_o = rest[3 * npg:3 * npg + 3]
    m_s, l_s, acc_s, gate_s = rest[3 * npg + 3:]
    j = pl.program_id(1)
    rows = 2 * N_HEADS
    row = lax.broadcasted_iota(I32, (rows, SEC), 0)
    lane = lax.broadcasted_iota(I32, (rows, SEC), 1)
    lane128 = lax.broadcasted_iota(I32, (rows, HEAD_W), 1)
    qa = qa_ref[...].astype(F32)
    qa16 = [jnp.broadcast_to(qa[:, hh * HEAD_W:(hh + 1) * HEAD_W], (rows, HEAD_W)).astype(BF16)
            for hh in range(N_HEADS)]
    qb_rows = jnp.where(lane // DH_B == row, qb_ref[...].astype(F32), 0.0)
    qb16 = qb_rows.astype(BF16)
    row128 = lax.broadcasted_iota(I32, (rows, HEAD_W), 0)

    def head_rows(page_ref, hh):
        return page_ref[pl.ds(hh, page_size, stride=N_HEADS), :].astype(BF16)
    scale_b = DH_B ** -0.5
    pages_per_block = MOBA_BLOCK // page_size

    @pl.when(j == 0)
    def _():
        m_s[...] = jnp.full(m_s.shape, NEG_INF, F32)
        l_s[...] = jnp.zeros(l_s.shape, F32)
        acc_s[...] = jnp.zeros(acc_s.shape, F32)
        gate_s[...] = jnp.zeros(gate_s.shape, F32)

    gate = gate_s[...]
    sb = []
    for p in range(npg):
        page = j * npg + p
        sa = jnp.zeros((rows, page_size), F32)
        for hh in range(N_HEADS):
            sa = jnp.where(row128 == hh, _nt(qa16[hh], head_rows(ka_pages[p], hh)), sa)
        sa_o[page] = sa
        gate = gate + jnp.where(lane128 == page // pages_per_block, jnp.sum(sa, axis=-1, keepdims=True), 0.0)
        sb.append(_dot(qb16, kb_pages[p][...].astype(BF16)) * scale_b)
    gate_s[...] = gate
    sb = jnp.concatenate(sb, axis=1)
    m_old = m_s[:, 0:1]
    m_new = jnp.maximum(m_old, jnp.max(sb, axis=-1, keepdims=True))
    alpha = jnp.exp(m_old - m_new)
    pf = jnp.exp(sb - m_new)
    pb = pf.astype(BF16)
    pv = []
    for hh in range(N_HEADS):
        pv_h = _dot(pb[:, 0:page_size], head_rows(vb_pages[0], hh))
        for p in range(1, npg):
            pv_h = pv_h + _dot(pb[:, p * page_size:(p + 1) * page_size], head_rows(vb_pages[p], hh))
        pv.append(pv_h)
    pv = jnp.concatenate(pv, axis=1)
    l_new = alpha * l_s[:, 0:1] + jnp.sum(pf, axis=-1, keepdims=True)
    acc = alpha * acc_s[...] + pv
    m_s[...] = jnp.broadcast_to(m_new, m_s.shape)
    l_s[...] = jnp.broadcast_to(l_new, l_s.shape)
    acc_s[...] = acc

    @pl.when(j == pl.num_programs(1) - 1)
    def _():
        s_new = jnp.sum(qb_rows * kbn_ref[...], axis=-1, keepdims=True) * scale_b
        m_f = jnp.maximum(m_new, s_new)
        a_f = jnp.exp(m_new - m_f)
        p_new = jnp.exp(s_new - m_f)
        o = (a_f * acc + p_new * vbn_ref[...]) / (a_f * l_new + p_new)
        coef = jnp.where(row % 2 == 0, 1.0, -lam_ref[0])
        ob = jnp.sum(jnp.where(lane // HEAD_W == row // 2, o * coef, 0.0), axis=0, keepdims=True)
        for hh in range(N_HEADS):
            sl = slice(hh * HEAD_W, (hh + 1) * HEAD_W)
            ob_o[:, sl] = _subln(ob[:, sl], gsub_ref[...]).astype(BF16)
        g = jnp.where(lane128 < n_past_blocks, gate * (1.0 / MOBA_BLOCK), NEG_INF)
        picks = jnp.zeros((rows, HEAD_W), I32)
        for t in range(MOBA_TOPK):
            mx = jnp.max(g, axis=-1, keepdims=True)
            idx = jnp.min(jnp.where(g == mx, lane128, HEAD_W), axis=-1, keepdims=True)
            picks = jnp.where(lane128 == t, idx, picks)
            g = jnp.where(lane128 == idx, NEG_INF, g)
        sel_o[...] = picks


def _sample_gather_kernel(page_ref, qa_ref, kan_ref, van_ref, sa_ref, va_hbm, oa_o, vbuf, sems, *, n, page_size):
    i = pl.program_id(0)
    pages_per_block = MOBA_BLOCK // page_size
    n_sel = N_HEADS * MOBA_TOPK * pages_per_block
    slot = i % 2

    def v_copy(seq, c, s):
        return pltpu.make_async_copy(va_hbm.at[page_ref[seq * 2 * n_sel + c]], vbuf.at[s, c], sems.at[s])

    def start_pages(seq, s):
        for c in range(n_sel):
            v_copy(seq, c, s).start()

    @pl.when(i == 0)
    def _():
        start_pages(0, 0)

    if n > 1:
        @pl.when(i + 1 < n)
        def _():
            start_pages(i + 1, 1 - slot)

    for c in range(n_sel):
        v_copy(i, c, slot).wait()

    scale_a = HEAD_W ** -0.5
    qa = qa_ref[...].astype(F32)
    for hh in range(N_HEADS):
        sl = slice(hh * HEAD_W, (hh + 1) * HEAD_W)
        c0 = hh * MOBA_TOPK * pages_per_block
        s = jnp.concatenate([sa_ref[page_ref[i * 2 * n_sel + n_sel + c0 + c], hh:hh + 1, :]
                             for c in range(MOBA_TOPK * pages_per_block)], axis=1) * scale_a
        s_self = jnp.sum(qa[:, sl] * kan_ref[:, sl], axis=-1, keepdims=True) * scale_a
        m = jnp.maximum(jnp.max(s, axis=-1, keepdims=True), s_self)
        p = jnp.exp(s - m)
        p_self = jnp.exp(s_self - m)
        l = jnp.sum(p, axis=-1, keepdims=True) + p_self
        p16 = jnp.broadcast_to(p, (8, p.shape[1])).astype(BF16)
        acc = p_self * van_ref[:, sl]
        for c in range(MOBA_TOPK * pages_per_block):
            v = vbuf[slot, c0 + c, pl.ds(hh, page_size, stride=N_HEADS), :].astype(BF16)
            acc = acc + _dot(p16[:, c * page_size:(c + 1) * page_size], v)[0:1, :]
        oa_o[:, sl] = (acc / l).astype(BF16)


def _attention_sample(lam, page_table, qa, qb, ka_new, va_new, kb_new, vb_new, gsub,
                      cache_ka, cache_va, cache_kb, cache_vb):
    ns, n_pages = page_table.shape
    n_pool, page_rows = cache_ka.shape[0], cache_ka.shape[1]
    page_size = page_rows // N_HEADS
    past = n_pages * page_size
    assert page_size == HEAD_W, "transposed diff-k pages share the (SEC, HEAD_W) page block shape"
    assert past % MOBA_BLOCK == 0 and MOBA_BLOCK % page_size == 0 and n_pages % PAGES_PER_STEP == 0
    n_past_blocks = past // MOBA_BLOCK
    assert MOBA_TOPK <= n_past_blocks <= HEAD_W
    pages_per_block = MOBA_BLOCK // page_size
    npg = PAGES_PER_STEP
    rows = 2 * N_HEADS
    pt_flat = page_table.reshape(-1)
    r3 = lambda a: a.reshape(ns, 1, SEC)
    tok = pl.BlockSpec((None, 1, SEC), lambda s, j, pt: (s, 0, 0))
    smem = pl.BlockSpec(memory_space=pltpu.SMEM)

    def page_spec(p):
        return pl.BlockSpec((None, page_rows, HEAD_W), lambda s, j, pt: (pt[s * n_pages + j * npg + p], 0, 0))

    pages = [page_spec(p) for p in range(npg)]
    scan = pl.pallas_call(
        functools.partial(_sample_scan_kernel, page_size=page_size, n_past_blocks=n_past_blocks),
        grid_spec=pltpu.PrefetchScalarGridSpec(
            num_scalar_prefetch=1, grid=(ns, n_pages // npg),
            in_specs=[smem, tok, tok, tok, tok, pl.BlockSpec((1, HEAD_W), lambda s, j, pt: (0, 0))] + pages * 3,
            out_specs=[pl.BlockSpec((None, n_pages, rows, page_size), lambda s, j, pt: (s, 0, 0, 0)),
                       pl.BlockSpec((None, rows, HEAD_W), lambda s, j, pt: (s, 0, 0)),
                       tok],
            scratch_shapes=[pltpu.VMEM((rows, HEAD_W), F32)] * 2 + [pltpu.VMEM((rows, SEC), F32),
                                                                    pltpu.VMEM((rows, HEAD_W), F32)]),
        out_shape=[jax.ShapeDtypeStruct((ns, n_pages, rows, page_size), F32),
                   jax.ShapeDtypeStruct((ns, rows, HEAD_W), I32),
                   jax.ShapeDtypeStruct((ns, 1, SEC), BF16)],
        compiler_params=pltpu.CompilerParams(dimension_semantics=("parallel", "arbitrary")),
        name="sample_scan",
    )
    sa_all, picks, ob = scan(pt_flat, lam, r3(qa), r3(qb), r3(kb_new), r3(vb_new), gsub,
                             *([cache_ka] * npg), *([cache_kb] * npg), *([cache_vb] * npg))
    sel_flat = picks[:, :N_HEADS, :MOBA_TOPK].reshape(-1)
    oa = _sample_gather(pt_flat, sel_flat, sa_all, qa, ka_new, va_new, cache_va, n_pages)
    return oa, ob.reshape(ns, SEC)


def _sample_gather(pt_flat, sel_flat, sa_all, qa, ka_new, va_new, cache_va, n_pages):
    ns = qa.shape[0]
    page_rows = cache_va.shape[1]
    page_size = page_rows // N_HEADS
    pages_per_block = MOBA_BLOCK // page_size
    rows = 2 * N_HEADS
    n_sel = N_HEADS * MOBA_TOPK * pages_per_block
    r3 = lambda a: a.reshape(ns, 1, SEC)
    tok1 = pl.BlockSpec((None, 1, SEC), lambda s, pg: (s, 0, 0))
    blocks = jnp.clip(sel_flat.reshape(ns, N_HEADS * MOBA_TOPK), 0, n_pages // pages_per_block - 1)
    pos = (blocks[:, :, None] * pages_per_block + jnp.arange(pages_per_block, dtype=I32)).reshape(ns, n_sel)
    pool = jnp.take_along_axis(pt_flat.reshape(ns, n_pages), pos, axis=1)
    page_tab = jnp.concatenate([pool, pos], axis=1).reshape(-1).astype(I32)
    gather = pl.pallas_call(
        functools.partial(_sample_gather_kernel, n=ns, page_size=page_size),
        grid_spec=pltpu.PrefetchScalarGridSpec(
            num_scalar_prefetch=1, grid=(ns,),
            in_specs=[tok1, tok1, tok1,
                      pl.BlockSpec((None, n_pages, rows, page_size), lambda s, pg: (s, 0, 0, 0)),
                      pl.BlockSpec(memory_space=pl.ANY)],
            out_specs=tok1,
            scratch_shapes=[pltpu.VMEM((2, n_sel, page_rows, HEAD_W), F32), pltpu.SemaphoreType.DMA((2,))]),
        out_shape=jax.ShapeDtypeStruct((ns, 1, SEC), BF16),
        compiler_params=pltpu.CompilerParams(dimension_semantics=("arbitrary",), vmem_limit_bytes=VMEM_LIMIT),
        name="sample_gather",
    )
    oa = gather(page_tab, r3(qa), r3(ka_new), r3(va_new), sa_all, cache_va)
    return oa.reshape(ns, SEC)


def _mem_sample_kernel(qm_ref, mk_ref, mv_ref, om_o):
    n_mem = mk_ref.shape[0] // N_HEADS
    qm = qm_ref[...].astype(F32)
    for hh in range(N_HEADS):
        sl = slice(hh * HEAD_W, (hh + 1) * HEAD_W)
        q16 = jnp.broadcast_to(qm[:, sl], (8, HEAD_W)).astype(BF16)
        k16 = mk_ref[pl.ds(hh, n_mem, stride=N_HEADS), :].astype(BF16)
        v16 = mv_ref[pl.ds(hh, n_mem, stride=N_HEADS), :].astype(BF16)
        _, l, acc = _softmax_start(_nt(q16, k16) * (HEAD_W ** -0.5), v16)
        om_o[:, sl] = (acc / l)[0:1, :].astype(BF16)


def _attention_mem_sample(qm, mk, mv):
    ns, mem_rows, _ = mk.shape
    tok = pl.BlockSpec((None, 1, SEC), lambda s: (s, 0, 0))
    kv = pl.BlockSpec((None, mem_rows, HEAD_W), lambda s: (s, 0, 0))
    return pl.pallas_call(
        _mem_sample_kernel, grid=(ns,), in_specs=[tok, kv, kv], out_specs=tok,
        out_shape=jax.ShapeDtypeStruct((ns, 1, SEC), BF16),
        compiler_params=pltpu.CompilerParams(dimension_semantics=("parallel",)),
        name="mem_sample",
    )(qm.reshape(ns, 1, SEC), mk, mv).reshape(ns, SEC)


def _merge_kernel(*refs, n_main, n_steps):
    i = pl.program_id(0)
    hn_o = refs[-3]

    @pl.when(i < n_main)
    def _():
        _merge_tile(*refs)

    if n_main < n_steps:
        @pl.when(i >= n_main)
        def _():
            hn_o[...] = jnp.zeros(hn_o.shape, hn_o.dtype)


def _merge_tile(x_ref, oa_ref, ob_ref, om_ref, g_ref, wa_ref, wb_ref, wm_ref, wo_ref, gffn_ref,
                wrh_ref, wrl_ref, br_ref, *rest):
    xmid_o, hn_o, idx_o, gate_o = rest[-4:]
    d = x_ref.shape[1]
    n_exp = wrh_ref.shape[0]
    merged = (g_ref[:, 0:d].astype(F32) * _dot(oa_ref[...], wa_ref[...])
              + g_ref[:, d:2 * d].astype(F32) * _dot(ob_ref[...], wb_ref[...])
              + g_ref[:, 2 * d:3 * d].astype(F32) * _dot(om_ref[...], wm_ref[...]))
    xm = x_ref[...] + _dot(merged.astype(BF16), wo_ref[...])
    xmid_o[...] = xm
    hn = _rms(xm, RMS_EPS) * gffn_ref[...]
    hn_o[...] = hn
    hi = hn.astype(BF16)
    lo = (hn - hi.astype(F32)).astype(BF16)
    g = _nt(wrh_ref[...], hi) + _nt(wrl_ref[...], hi) + _nt(wrh_ref[...], lo) + br_ref[...]
    erow = lax.broadcasted_iota(I32, g.shape, 0)
    vals, idxs = [], []
    for _ in range(MOE_TOPK):
        mx = jnp.max(g, axis=0, keepdims=True)
        ix = jnp.min(jnp.where(g == mx, erow, n_exp), axis=0, keepdims=True)
        vals.append(mx)
        idxs.append(ix)
        g = jnp.where(erow == ix, NEG_INF, g)
    e = jnp.exp(jnp.concatenate(vals, axis=0) - vals[0])
    gate_o[...] = e / jnp.sum(e, axis=0, keepdims=True)
    idx_o[...] = jnp.concatenate(idxs, axis=0)


def _merge(x2d, oa, ob, om, g, tm, w_a, w_b, w_m, w_o, gffn, wr_hi, wr_lo, br, hn_all, row_block0):
    n, d = x2d.shape
    n_exp = wr_hi.shape[0]
    n_main = n // tm
    creates = isinstance(hn_all, tuple)
    n_all = hn_all[0] if creates else hn_all.shape[0]
    n_steps = pl.cdiv(n_all, tm) if creates else n_main
    last = n_main - 1
    row = lambda w: pl.BlockSpec((tm, w), lambda i: (jnp.minimum(i, last), 0))
    col = pl.BlockSpec((MOE_TOPK, tm), lambda i: (0, jnp.minimum(i, last)))
    in_specs = [row(d), row(SEC), row(SEC), row(SEC), row(3 * d),
                _const_spec((SEC, d)), _const_spec((SEC, d)), _const_spec((SEC, d)), _const_spec((d, d)),
                _const_spec((1, d)), _const_spec((n_exp, d)), _const_spec((n_exp, d)), _const_spec((n_exp, 1))]
    args = [x2d, oa, ob, om, g, w_a, w_b, w_m, w_o, gffn, wr_hi, wr_lo, br]
    aliases = {}
    if not creates:
        in_specs.append(pl.BlockSpec(memory_space=pl.ANY))
        args.append(hn_all)
        aliases = {len(args) - 1: 1}
    return pl.pallas_call(
        functools.partial(_merge_kernel, n_main=n_main, n_steps=n_steps), grid=(n_steps,), in_specs=in_specs,
        out_specs=[row(d), pl.BlockSpec((tm, d), lambda i: (row_block0 + i, 0)), col, col],
        out_shape=[jax.ShapeDtypeStruct((n, d), F32), jax.ShapeDtypeStruct((n_all, d), F32),
                   jax.ShapeDtypeStruct((MOE_TOPK, n), I32), jax.ShapeDtypeStruct((MOE_TOPK, n), F32)],
        input_output_aliases=aliases,
        compiler_params=pltpu.CompilerParams(dimension_semantics=("arbitrary",), vmem_limit_bytes=VMEM_LIMIT),
        name="merge",
    )(*args)


ROW_COPY_UNROLL = 8
EXPERT_ISSUE_PIECES = 4


def _start_row_copies(src_hbm, idx_ref, nrow, dst, sem):
    def body(g, c):
        for u in range(ROW_COPY_UNROLL):
            r = g * ROW_COPY_UNROLL + u
            pltpu.make_async_copy(src_hbm.at[pl.ds(idx_ref[0, r], 1), :], dst.at[pl.ds(r, 1), :], sem).start()
        return c
    lax.fori_loop(0, nrow // ROW_COPY_UNROLL, body, 0)


def _expert_kernel(be_ref, src_ref, nxt_ref, gate_ref, hn_hbm, w1_ref, b1_ref, w2_ref, b2_ref, y_o, xbuf, sems,
                   *, n):
    i = pl.program_id(0)
    nrow = xbuf.shape[1]
    d_ff = w2_ref.shape[0]
    slot = i % 2

    @pl.when(i == 0)
    def _():
        _start_row_copies(hn_hbm, src_ref, nrow, xbuf.at[0], sems.at[0])

    def slot_wait(s):
        pltpu.make_async_copy(hn_hbm.at[pl.ds(0, nrow), :], xbuf.at[s], sems.at[s]).wait()

    slot_wait(slot)
    x = xbuf[slot].astype(BF16)

    rows_per_piece = nrow // EXPERT_ISSUE_PIECES
    cols_per_piece = 2 * d_ff // EXPERT_ISSUE_PIECES
    pieces = []
    for c in range(EXPERT_ISSUE_PIECES):
        for r in range(c * rows_per_piece, (c + 1) * rows_per_piece):
            pltpu.make_async_copy(hn_hbm.at[pl.ds(nxt_ref[0, r], 1), :], xbuf.at[1 - slot, pl.ds(r, 1), :],
                                  sems.at[1 - slot]).start()
        cols = slice(c * cols_per_piece, (c + 1) * cols_per_piece)
        pieces.append(_dot(x, w1_ref[:, cols]) + b1_ref[:, cols])
    u = jnp.concatenate(pieces, axis=1)

    @pl.when(i == n - 1)
    def _():
        slot_wait(1 - slot)

    u_glu = jnp.minimum(u[:, :d_ff], SWIGLU_LIMIT)
    u_lin = jnp.clip(u[:, d_ff:], -SWIGLU_LIMIT, SWIGLU_LIMIT)
    act = u_glu * (1.0 / (1.0 + jnp.exp(-SWIGLU_ALPHA * u_glu))) * (u_lin + 1.0)
    y = _dot(act.astype(BF16), w2_ref[...]) + b2_ref[...]
    eye = lax.broadcasted_iota(I32, (nrow, nrow), 0) == lax.broadcasted_iota(I32, (nrow, nrow), 1)
    gcol = jnp.sum(jnp.where(eye, gate_ref[...], 0.0), axis=-1, keepdims=True)
    y_o[...] = y * gcol


def _experts(blk_expert, row_src, row_gate, hn_all, w1, b1, w2, b2):
    n_blocks = blk_expert.shape[0]
    n_exp, d, f2 = w1.shape
    d_ff = f2 // 2
    src3 = row_src.reshape(n_blocks, 1, MOE_BLOCK)
    gate3 = row_gate.reshape(n_blocks, 1, MOE_BLOCK)
    idx_spec = lambda off: pl.BlockSpec((None, 1, MOE_BLOCK),
                                        lambda i, be: (jnp.minimum(i + off, n_blocks - 1), 0, 0),
                                        memory_space=pltpu.SMEM)
    return pl.pallas_call(
        functools.partial(_expert_kernel, n=n_blocks),
        grid_spec=pltpu.PrefetchScalarGridSpec(
            num_scalar_prefetch=1, grid=(n_blocks,),
            in_specs=[idx_spec(0), idx_spec(1),
                      pl.BlockSpec((None, 1, MOE_BLOCK), lambda i, be: (i, 0, 0)),
                      pl.BlockSpec(memory_space=pl.ANY),
                      pl.BlockSpec((None, d, f2), lambda i, be: (be[i], 0, 0)),
                      pl.BlockSpec((None, 1, f2), lambda i, be: (be[i], 0, 0)),
                      pl.BlockSpec((None, d_ff, d), lambda i, be: (be[i], 0, 0)),
                      pl.BlockSpec((None, 1, d), lambda i, be: (be[i], 0, 0))],
            out_specs=pl.BlockSpec((MOE_BLOCK, d), lambda i, be: (i, 0)),
            scratch_shapes=[pltpu.VMEM((2, MOE_BLOCK, d), F32), pltpu.SemaphoreType.DMA((2,))]),
        out_shape=jax.ShapeDtypeStruct((n_blocks * MOE_BLOCK, d), F32),
        compiler_params=pltpu.CompilerParams(dimension_semantics=("arbitrary",), vmem_limit_bytes=VMEM_LIMIT),
        name="experts",
    )(blk_expert, src3, src3, gate3, hn_all, w1, b1.reshape(n_exp, 1, f2), w2, b2.reshape(n_exp, 1, d))


def _combine_kernel(pos_ref, nxt_ref, xmid_ref, ys_hbm, y_o, buf, sems, *, n):
    i = pl.program_id(0)
    tm = xmid_ref.shape[0]
    nrow = MOE_TOPK * tm
    slot = i % 2

    def start_rows(idx_ref, s):
        _start_row_copies(ys_hbm, idx_ref, nrow, buf.at[s], sems.at[s])

    @pl.when(i == 0)
    def _():
        start_rows(pos_ref, 0)

    if n > 1:
        @pl.when(i + 1 < n)
        def _():
            start_rows(nxt_ref, 1 - slot)

    pltpu.make_async_copy(ys_hbm.at[pl.ds(0, nrow), :], buf.at[slot], sems.at[slot]).wait()
    acc = xmid_ref[...]
    for k in range(MOE_TOPK):
        acc = acc + buf[slot, k * tm:(k + 1) * tm, :]
    y_o[...] = acc


def _combine(pos_kt, xmid, ys, tm):
    n, d = xmid.shape
    n_tiles = n // tm
    pos3 = pos_kt.reshape(MOE_TOPK, n_tiles, tm).transpose(1, 0, 2).reshape(n_tiles, 1, MOE_TOPK * tm)
    idx_spec = lambda off: pl.BlockSpec((None, 1, MOE_TOPK * tm),
                                        lambda i: (jnp.minimum(i + off, n_tiles - 1), 0, 0),
                                        memory_space=pltpu.SMEM)
    return pl.pallas_call(
        functools.partial(_combine_kernel, n=n_tiles), grid=(n_tiles,),
        in_specs=[idx_spec(0), idx_spec(1), pl.BlockSpec((tm, d), lambda i: (i, 0)),
                  pl.BlockSpec(memory_space=pl.ANY)],
        out_specs=pl.BlockSpec((tm, d), lambda i: (i, 0)),
        out_shape=jax.ShapeDtypeStruct((n, d), F32),
        scratch_shapes=[pltpu.VMEM((2, MOE_TOPK * tm, d), F32), pltpu.SemaphoreType.DMA((2,))],
        compiler_params=pltpu.CompilerParams(dimension_semantics=("arbitrary",)),
        name="combine",
    )(pos3, pos3, xmid, ys)


def _route(e_tk, g_tk, n_exp):
    n_tok = e_tk.shape[0]
    n_as = n_tok * MOE_TOPK
    e_flat = e_tk.reshape(n_as)
    g_flat = g_tk.reshape(n_as)
    order = jnp.argsort(e_flat, stable=True)
    e_sorted = e_flat[order]
    counts = jnp.bincount(e_flat, length=n_exp)
    padded = (counts + MOE_BLOCK - 1) // MOE_BLOCK * MOE_BLOCK
    pad_end = jnp.cumsum(padded)
    pad_start = pad_end - padded
    start = jnp.cumsum(counts) - counts
    dest = (pad_start[e_sorted] + jnp.arange(n_as, dtype=I32) - start[e_sorted]).astype(I32)
    n_blocks = -(-n_as // MOE_BLOCK) + n_exp
    blk_start = jnp.arange(n_blocks, dtype=I32) * MOE_BLOCK
    blk_expert = jnp.minimum(jnp.sum(pad_end[None, :] <= blk_start[:, None], axis=1), n_exp - 1).astype(I32)
    within = (blk_start - pad_start[blk_expert].astype(I32))[:, None] + jnp.arange(MOE_BLOCK, dtype=I32)[None, :]
    valid = within < counts[blk_expert].astype(I32)[:, None]
    src = jnp.clip(within + start[blk_expert].astype(I32)[:, None], 0, n_as - 1).reshape(-1)
    assign = order[src].astype(I32)
    valid = valid.reshape(-1)
    row_src = jnp.where(valid, assign // MOE_TOPK, 0)
    row_gate = jnp.where(valid, g_flat[assign], 0.0)
    _, pos = lax.sort_key_val(order.astype(I32), dest)
    return blk_expert, row_src, row_gate, pos.reshape(n_tok, MOE_TOPK).T


def kernel(x_prompt, x_sample, mem_prompt, cache_moba_k, cache_moba_v, cache_diff_k, cache_diff_v, cache_mem_k, cache_mem_v, page_table, w_in, b_gate, g_norm_mix, g_qa, g_ka, g_qb, g_kb, lambda_q1, lambda_k1, lambda_q2, lambda_k2, g_subln, g_norm_mem, w_mem_kv, g_qm, g_km, w_br_a, w_br_b, w_br_m, w_out, g_norm_ffn, w_router, b_router, w1, b1, w2, b2):
    assert w_in.shape[0] == 1, "single-layer trunk"
    batch, seq, d = x_prompt.shape
    ns, dec_seq, _ = x_sample.shape
    assert dec_seq == 1 and seq % MOBA_BLOCK == 0 and seq // MOBA_BLOCK >= MOBA_TOPK
    n_mem = mem_prompt.shape[1]
    n_pages, page_size = page_table.shape[1], cache_moba_k.shape[2]
    n_pool = cache_moba_k.shape[1]
    np_tok = batch * seq
    n_exp = w_router.shape[-1]
    row1 = lambda a: a.reshape(1, -1).astype(F32)

    lam = (jnp.exp(jnp.sum(lambda_q1[0].astype(F32) * lambda_k1[0].astype(F32)))
           - jnp.exp(jnp.sum(lambda_q2[0].astype(F32) * lambda_k2[0].astype(F32))) + LAM_INIT).reshape(1)
    w_in16 = w_in[0].astype(BF16)
    gqb2, gkb2 = (row1(jnp.tile(g[0], HEAD_W // DH_B)) for g in (g_qb, g_kb))
    proj_w = (w_in16, row1(g_norm_mix), row1(b_gate), row1(g_qa), row1(g_ka), gqb2, gkb2, row1(g_qm))
    gsub = row1(g_subln)

    tm_p = 512 if seq % 512 == 0 else MOBA_BLOCK
    pos_p = jnp.arange(seq, dtype=I32)
    qa, ka, va, qb, kb, vb, qm, gates = _project(
        x_prompt.reshape(np_tok, d), tm_p, seq // tm_p, *proj_w, _rope_tables(pos_p, HEAD_W), _rope_tables(pos_p, DH_B))
    mk, mv = _memory_kv(mem_prompt.reshape(batch * n_mem, d), min(512, batch * n_mem), w_mem_kv[0].astype(BF16),
                        row1(g_norm_mem), row1(g_km))
    oa, ob, om = _attention_prompt(lam, qa, qb, qm, ka, va, kb, vb, mk, mv, gsub, batch, seq, n_mem)

    pos_s = jnp.full((ns,), n_pages * page_size, dtype=I32)
    qa_s, ka_s, va_s, qb_s, kb_s, vb_s, qm_s, gates_s = _project(
        x_sample.reshape(ns, d), ns, 1, *proj_w, _rope_tables(pos_s, HEAD_W), _rope_tables(pos_s, DH_B))
    key_head_rows = lambda c: c.reshape(c.shape[1], c.shape[2] * N_HEADS, HEAD_W)
    diff_kt = jnp.transpose(cache_diff_k[0], (0, 2, 3, 4, 1)).reshape(n_pool, SEC, page_size)
    oa_s, ob_s = _attention_sample(lam, page_table, qa_s, qb_s, ka_s, va_s, kb_s, vb_s, gsub,
                                   key_head_rows(cache_moba_k), key_head_rows(cache_moba_v),
                                   diff_kt, key_head_rows(cache_diff_v))
    om_s = _attention_mem_sample(qm_s, key_head_rows(cache_mem_k), key_head_rows(cache_mem_v))

    wr = w_router[0].T.astype(F32)
    wr_hi = wr.astype(BF16)
    wr_lo = (wr - wr_hi.astype(F32)).astype(BF16)
    merge_w = (w_br_a[0].astype(BF16), w_br_b[0].astype(BF16), w_br_m[0].astype(BF16), w_out[0].astype(BF16),
               row1(g_norm_ffn), wr_hi, wr_lo, b_router[0].reshape(n_exp, 1).astype(F32))
    n_all = np_tok + ns
    xmid_p, hn_all, e_p, g_p = _merge(x_prompt.reshape(np_tok, d), oa, ob, om, gates, tm_p, *merge_w, (n_all,), 0)
    xmid_s, hn_all, e_s, g_s = _merge(x_sample.reshape(ns, d), oa_s, ob_s, om_s, gates_s, ns, *merge_w, hn_all,
                                      np_tok // ns)

    e_tk = jnp.concatenate([e_p, e_s], axis=1).T
    g_tk = jnp.concatenate([g_p, g_s], axis=1).T
    blk_expert, row_src, row_gate, pos_kt = _route(e_tk, g_tk, n_exp)
    ys = _experts(blk_expert, row_src, row_gate, hn_all, w1[0].astype(BF16), b1[0].astype(F32),
                  w2[0].astype(BF16), b2[0].astype(F32))
    y_p = _combine(pos_kt[:, :np_tok], xmid_p, ys, 128)
    y_s = _combine(pos_kt[:, np_tok:], xmid_s, ys, ns)

    k5 = lambda a, n: a.reshape(1, n, -1, N_HEADS, HEAD_W)
    k6 = lambda a, n: a.reshape(1, n, -1, N_HEADS, 2, DH_B)
    return (y_p.reshape(batch, seq, d), y_s.reshape(ns, 1, d),
            k5(ka, batch), k5(va, batch), k6(kb, batch), k5(vb, batch),
            k5(mk, batch), k5(mv, batch),
            k5(ka_s, ns), k5(va_s, ns), k6(kb_s, ns), k5(vb_s, ns))
```
